```python
import jax, jax.numpy as jnp
from jax import lax
import numpy as np

D_MODEL = 1024
BATCH = 8
SEQ = 4096
DEPTH = 4

CHUNK = 64
N_MEM = 256
EPS = 1e-6
GLA_HEADS = 4
GLA_DK = 64
GLA_DV = 128
GLA_WIDTH = GLA_HEADS * GLA_DV
GLA_KW = GLA_HEADS * GLA_DK
GATE_RANK = 16
GATE_TAU = 16.0
SG_WIDTH = D_MODEL - GLA_WIDTH
SG_GROUPS = 4
SG_GROUP_CH = SG_WIDTH // SG_GROUPS
SG_BLOCK = 128
X_HEADS = 4
X_HEAD_DIM = D_MODEL // X_HEADS
D_FF = -(-8 * D_MODEL // (3 * 256)) * 256
OFF_Q = 0
OFF_K = OFF_Q + GLA_KW
OFF_V = OFF_K + GLA_KW
OFF_G = OFF_V + GLA_WIDTH
OFF_A = OFF_G + GLA_WIDTH
OFF_U = OFF_A + GATE_RANK
OFF_SV = OFF_U + SG_WIDTH
D_IN = OFF_SV + SG_WIDTH

kernel_name = "hybrid_gla_gmlp_memxattn_sandwich"


def rmsnorm(x, g):
    xf = x.astype(jnp.float32)
    y = xf * lax.rsqrt(jnp.mean(xf * xf, axis=-1, keepdims=True) + EPS)
    return (y * g.astype(jnp.float32)).astype(x.dtype)


def layernorm(x, g, b):
    xf = x.astype(jnp.float32)
    mu = jnp.mean(xf, axis=-1, keepdims=True)
    var = jnp.mean(jnp.square(xf - mu), axis=-1, keepdims=True)
    y = (xf - mu) * lax.rsqrt(var + EPS)
    return (y * g.astype(jnp.float32) + b.astype(jnp.float32)).astype(x.dtype)


def gla_mixer(q, k, v, g, a_lr, w_gate_up, b_gate, onorm):
    B, S, _ = q.shape
    nc = S // CHUNK
    qc = q.reshape(B, nc, CHUNK, GLA_HEADS, GLA_DK).astype(jnp.float32) * (GLA_DK ** -0.5)
    kc = k.reshape(B, nc, CHUNK, GLA_HEADS, GLA_DK).astype(jnp.float32)
    vc = v.reshape(B, nc, CHUNK, GLA_HEADS, GLA_DV).astype(jnp.float32)
    loga = jax.nn.log_sigmoid((a_lr @ w_gate_up + b_gate).astype(jnp.float32)) / GATE_TAU
    loga = loga.reshape(B, nc, CHUNK, GLA_HEADS, GLA_DK)
    cum = jnp.cumsum(loga, axis=2)
    cum_end = cum[:, :, -1:]
    k_dec = kc * jnp.exp(cum_end - cum)
    u_c = jnp.einsum('bnchk,bnchv->nbhkv', k_dec, vc)
    a_c = jnp.moveaxis(jnp.exp(cum_end[:, :, 0]), 1, 0)

    def step(state, inp):
        a, u = inp
        state = a[..., None] * state + u
        return state, state

    s0 = jnp.zeros((B, GLA_HEADS, GLA_DK, GLA_DV), jnp.float32)
    _, states = lax.scan(step, s0, (a_c, u_c))
    o = jnp.einsum('bnchk,nbhkv->bnchv', qc, states).reshape(B, S, GLA_HEADS, GLA_DV)
    o = o * lax.rsqrt(jnp.mean(o * o, axis=-1, keepdims=True) + EPS)
    o = o.reshape(B, S, GLA_WIDTH) * onorm.astype(jnp.float32)
    return (o * jax.nn.silu(g.astype(jnp.float32))).astype(q.dtype)


def spatial_gating_mixer(u, sv, ln_g, ln_b, w_s, b_s):
    B, S, _ = u.shape
    nb = S // SG_BLOCK
    u = jax.nn.gelu(u)
    sv = layernorm(jax.nn.gelu(sv), ln_g, ln_b)
    pos = jnp.arange(SG_BLOCK) // CHUNK
    mask = (pos[None, :] <= pos[:, None]).astype(w_s.dtype)
    svb = sv.reshape(B, nb, SG_BLOCK, SG_GROUPS, SG_GROUP_CH)
    z = jnp.einsum('gij,bnjgc->bnigc', w_s * mask, svb) + b_s.T[None, None, :, :, None]
    return u * z.reshape(B, S, SG_WIDTH)


def memory_cross_attention(xn, memn, w_xq, w_xkv, w_xo):
    B, S, _ = xn.shape
    q = (xn @ w_xq).reshape(B, S, X_HEADS, X_HEAD_DIM)
    kv = memn @ w_xkv
    k = kv[..., :D_MODEL].reshape(B, N_MEM, X_HEADS, X_HEAD_DIM)
    v = kv[..., D_MODEL:].reshape(B, N_MEM, X_HEADS, X_HEAD_DIM)
    s = jnp.einsum('bshd,bmhd->bhsm', q, k).astype(jnp.float32) * (X_HEAD_DIM ** -0.5)
    p = jax.nn.softmax(s, axis=-1).astype(v.dtype)
    o = jnp.einsum('bhsm,bmhd->bshd', p, v).reshape(B, S, D_MODEL)
    return o @ w_xo


def swiglu_ffn(xn, w_gu, w_down):
    h = xn @ w_gu
    return (jax.nn.silu(h[..., :D_FF]) * h[..., D_FF:]) @ w_down


def setup_inputs(seed: int = 0) -> dict:
    key = jax.random.key(seed)
    ks = jax.random.split(key, 24)
    L = DEPTH

    def w(k, shape, fan_in):
        return jax.random.normal(k, shape, jnp.float32) * (fan_in ** -0.5)

    def gain(k, shape):
        return 1.0 + 0.02 * jax.random.normal(k, shape, jnp.float32)

    def small(k, shape):
        return 0.02 * jax.random.normal(k, shape, jnp.float32)

    return {
        "x": jax.random.normal(ks[0], (BATCH, SEQ, D_MODEL), jnp.float32),
        "mem": jax.random.normal(ks[1], (BATCH, N_MEM, D_MODEL), jnp.float32),
        "norm_mix_pre": gain(ks[2], (L, D_MODEL)),
        "w_in": w(ks[3], (L, D_MODEL, D_IN), D_MODEL),
        "w_gate_up": w(ks[4], (L, GATE_RANK, GLA_KW), GATE_RANK),
        "b_gate": 0.5 + 0.1 * jax.random.normal(ks[5], (L, GLA_KW), jnp.float32),
        "gla_onorm": gain(ks[6], (L, GLA_WIDTH)),
        "sg_ln_g": gain(ks[7], (L, SG_WIDTH)),
        "sg_ln_b": small(ks[8], (L, SG_WIDTH)),
        "w_spatial": 0.5 * w(ks[9], (L, SG_GROUPS, SG_BLOCK, SG_BLOCK), SG_BLOCK),
        "b_spatial": gain(ks[10], (L, SG_GROUPS, SG_BLOCK)),
        "w_out": w(ks[11], (L, GLA_WIDTH + SG_WIDTH, D_MODEL), GLA_WIDTH + SG_WIDTH),
        "norm_mix_post": gain(ks[12], (L, D_MODEL)),
        "norm_x_pre": gain(ks[13], (L, D_MODEL)),
        "norm_mem": gain(ks[14], (L, D_MODEL)),
        "w_xq": w(ks[15], (L, D_MODEL, D_MODEL), D_MODEL),
        "w_xkv": w(ks[16], (L, D_MODEL, 2 * D_MODEL), D_MODEL),
        "w_xo": w(ks[17], (L, D_MODEL, D_MODEL), D_MODEL),
        "norm_x_post": gain(ks[18], (L, D_MODEL)),
        "norm_ffn_pre": gain(ks[19], (L, D_MODEL)),
        "w_ffn_gu": w(ks[20], (L, D_MODEL, 2 * D_FF), D_MODEL),
        "w_ffn_down": w(ks[21], (L, D_FF, D_MODEL), D_FF),
        "norm_ffn_post": gain(ks[22], (L, D_MODEL)),
    }


def reference(x, mem, norm_mix_pre, w_in, w_gate_up, b_gate, gla_onorm, sg_ln_g, sg_ln_b,
              w_spatial, b_spatial, w_out, norm_mix_post, norm_x_pre, norm_mem, w_xq, w_xkv,
              w_xo, norm_x_post, norm_ffn_pre, w_ffn_gu, w_ffn_down, norm_ffn_post):
    for l in range(DEPTH):
        xn = rmsnorm(x, norm_mix_pre[l])
        h = xn @ w_in[l]
        gla_out = gla_mixer(h[..., OFF_Q:OFF_K], h[..., OFF_K:OFF_V], h[..., OFF_V:OFF_G],
                            h[..., OFF_G:OFF_A], h[..., OFF_A:OFF_U],
                            w_gate_up[l], b_gate[l], gla_onorm[l])
        sg_out = spatial_gating_mixer(h[..., OFF_U:OFF_SV], h[..., OFF_SV:D_IN],
                                      sg_ln_g[l], sg_ln_b[l], w_spatial[l], b_spatial[l])
        mix = jnp.concatenate([gla_out, sg_out], axis=-1) @ w_out[l]
        x = x + rmsnorm(mix, norm_mix_post[l])
        xa = memory_cross_attention(rmsnorm(x, norm_x_pre[l]), rmsnorm(mem, norm_mem[l]),
                                    w_xq[l], w_xkv[l], w_xo[l])
        x = x + rmsnorm(xa, norm_x_post[l])
        f = swiglu_ffn(rmsnorm(x, norm_ffn_pre[l]), w_ffn_gu[l], w_ffn_down[l])
        x = x + rmsnorm(f, norm_ffn_post[l])
    return x
```

```python
import functools

import numpy as np
import jax
import jax.numpy as jnp
from jax import lax
from jax.experimental import pallas as pl
from jax.experimental.pallas import tpu as pltpu

D_MODEL = 1024
CHUNK = 64
N_MEM = 256
EPS = 1e-6
GLA_HEADS = 4
GLA_DK = 64
GLA_DV = 128
GLA_WIDTH = GLA_HEADS * GLA_DV
GLA_KW = GLA_HEADS * GLA_DK
GATE_RANK = 16
GATE_TAU = 16.0
SG_WIDTH = D_MODEL - GLA_WIDTH
SG_GROUPS = 4
SG_GROUP_CH = SG_WIDTH // SG_GROUPS
SG_BLOCK = 128
X_HEADS = 4
X_HEAD_DIM = D_MODEL // X_HEADS
D_FF = 2816
OFF_Q = 0
OFF_K = OFF_Q + GLA_KW
OFF_V = OFF_K + GLA_KW
OFF_G = OFF_V + GLA_WIDTH
OFF_A = OFF_G + GLA_WIDTH
OFF_U = OFF_A + GATE_RANK
OFF_SV = OFF_U + SG_WIDTH
D_IN = OFF_SV + SG_WIDTH

LANES = 128
GATE_PAD = LANES
M_Q = 0
M_V = M_Q + GLA_KW
M_G = M_V + GLA_WIDTH
M_U = M_G + GLA_WIDTH
M_SV = M_U + SG_WIDTH
M_A = M_SV + SG_WIDTH
D_MAIN = M_A + GATE_PAD

TM = 512
VMEM_LIMIT = 56 * 1024 * 1024

F32 = jnp.float32
BF16 = jnp.bfloat16
NT_DIMS = (((1,), (1,)), ((), ()))


def _rms(x, g):
    ms = jnp.mean(x * x, axis=-1, keepdims=True)
    return x * lax.rsqrt(ms + EPS) * g


def _gelu_tanh(x):
    c = np.float32(np.sqrt(2.0 / np.pi))
    return 0.5 * x * (1.0 + jnp.tanh(c * (x + np.float32(0.044715) * (x * x * x))))


def _dot(a, b):
    return jnp.dot(a, b, preferred_element_type=F32)


def _mixer_kernel(x_ref, gpre_ref, wmain_ref, wkt_ref, wgt_ref, bg_ref, mm_ref, onorm_ref,
                  lng_ref, lnb_ref, ws_ref, bsb_ref, wout_ref, gpost_ref, o_ref,
                  s_ref, sbd_ref, gla_ref, sg_ref):
    n_chunks = TM // CHUNK

    @pl.when(pl.program_id(1) == 0)
    def _():
        s_ref[...] = jnp.zeros_like(s_ref)
        sbd_ref[...] = jnp.zeros_like(sbd_ref)

    x = x_ref[0]
    xn = _rms(x, gpre_ref[...]).astype(BF16)
    h = _dot(xn, wmain_ref[...])
    k_t = lax.dot_general(wkt_ref[...], xn, NT_DIMS, preferred_element_type=F32)

    q = h[:, M_Q:M_Q + GLA_KW].astype(BF16)
    v = h[:, M_V:M_V + GLA_WIDTH].astype(BF16)
    a_lr = h[:, M_A:M_A + GATE_PAD].astype(BF16)

    lane = lax.broadcasted_iota(jnp.int32, (GLA_KW, LANES), 1)
    first_chunk = lane < CHUNK
    mm = mm_ref[...]
    state = [s_ref[GLA_DK * hh:GLA_DK * (hh + 1), :] for hh in range(GLA_HEADS)]

    for j in range(TM // LANES):
        r = slice(LANES * j, LANES * (j + 1))
        z_t = lax.dot_general(wgt_ref[...], a_lr[r], NT_DIMS, preferred_element_type=F32) + bg_ref[...]
        la_t = (jnp.minimum(z_t, 0.0) - jnp.log(1.0 + jnp.exp(-jnp.abs(z_t)))) * np.float32(1.0 / GATE_TAU)
        la_hi = la_t.astype(BF16)
        la_lo = (la_t - la_hi.astype(F32)).astype(BF16)
        ex = jnp.exp(_dot(la_hi, mm) + _dot(la_lo, mm))
        kd = k_t[:, r] * ex[:, :LANES]
        kd_a = jnp.where(first_chunk, kd, 0.0).astype(BF16)
        kd_b = jnp.where(first_chunk, 0.0, kd).astype(BF16)
        inc = _dot(jnp.concatenate([kd_a, kd_b], axis=0), v[r])
        for c2 in range(2):
            c = 2 * j + c2
            tot = ex[:, LANES * (1 + c2):LANES * (2 + c2)]
            for hh in range(GLA_HEADS):
                rs = slice(GLA_DK * hh, GLA_DK * (hh + 1))
                cs = slice(GLA_DV * hh, GLA_DV * (hh + 1))
                state[hh] = tot[rs] * state[hh] + inc[GLA_KW * c2 + GLA_DK * hh:GLA_KW * c2 + GLA_DK * (hh + 1), cs]
                sbd_ref[c, rs, cs] = state[hh].astype(BF16)
            gla_ref[CHUNK * c:CHUNK * (c + 1), :] = _dot(q[CHUNK * c:CHUNK * (c + 1)], sbd_ref[c])

    for hh in range(GLA_HEADS):
        s_ref[GLA_DK * hh:GLA_DK * (hh + 1), :] = state[hh]

    g = h[:, M_G:M_G + GLA_WIDTH]
    o = gla_ref[...] * np.float32(GLA_DK ** -0.5)
    heads = []
    for hh in range(GLA_HEADS):
        oh = o[:, GLA_DV * hh:GLA_DV * (hh + 1)]
        heads.append(oh * lax.rsqrt(jnp.mean(oh * oh, axis=-1, keepdims=True) + EPS))
    o = jnp.concatenate(heads, axis=-1) * onorm_ref[...]
    gla_out = (o * (g / (1.0 + jnp.exp(-g)))).astype(BF16)

    u = _gelu_tanh(h[:, M_U:M_U + SG_WIDTH])
    sv = _gelu_tanh(h[:, M_SV:M_SV + SG_WIDTH])
    mu = jnp.mean(sv, axis=-1, keepdims=True)
    svc = sv - mu
    var = jnp.mean(svc * svc, axis=-1, keepdims=True)
    svn = (svc * lax.rsqrt(var + EPS) * lng_ref[...] + lnb_ref[...]).astype(BF16)
    row_chunk = lax.broadcasted_iota(jnp.int32, (SG_BLOCK, SG_BLOCK), 0) // CHUNK
    col_chunk = lax.broadcasted_iota(jnp.int32, (SG_BLOCK, SG_BLOCK), 1) // CHUNK
    causal = col_chunk <= row_chunk
    for gi in range(SG_GROUPS):
        cs = slice(SG_GROUP_CH * gi, SG_GROUP_CH * (gi + 1))
        w_g = jnp.where(causal, ws_ref[gi], jnp.zeros((), BF16))
        for j in range(TM // SG_BLOCK):
            r = slice(SG_BLOCK * j, SG_BLOCK * (j + 1))
            z = _dot(w_g, svn[r, cs]) + bsb_ref[:, cs]
            sg_ref[r, cs] = (u[r, cs] * z).astype(BF16)

    mix = _dot(gla_out, wout_ref[0:GLA_WIDTH, :]) + _dot(sg_ref[...], wout_ref[GLA_WIDTH:, :])
    o_ref[0] = x + _rms(mix, gpost_ref[...])


def _kv_kernel(mem_ref, gmem_ref, wkv_ref, k_ref, v_ref):
    memn = _rms(mem_ref[0], gmem_ref[...]).astype(BF16)
    kv = _dot(memn, wkv_ref[...])
    k_ref[0] = kv[:, :D_MODEL].astype(BF16)
    v_ref[0] = kv[:, D_MODEL:].astype(BF16)


def _attn_kernel(x_ref, gpre_ref, wq_ref, k_ref, v_ref, wo_ref, gpost_ref, o_ref):
    x = x_ref[0]
    xn = _rms(x, gpre_ref[...]).astype(BF16)
    q = _dot(xn, wq_ref[...]).astype(BF16)
    outs = []
    for hh in range(X_HEADS):
        cs = slice(X_HEAD_DIM * hh, X_HEAD_DIM * (hh + 1))
        s = lax.dot_general(q[:, cs], k_ref[0, :, cs], NT_DIMS, preferred_element_type=F32)
        s = s * np.float32(X_HEAD_DIM ** -0.5)
        p = jnp.exp(s - jnp.max(s, axis=-1, keepdims=True))
        denom = jnp.sum(p, axis=-1, keepdims=True)
        outs.append((_dot(p.astype(BF16), v_ref[0, :, cs]) / denom).astype(BF16))
    xa = _dot(jnp.concatenate(outs, axis=-1), wo_ref[...])
    o_ref[0] = x + _rms(xa, gpost_ref[...])


def _ffn_kernel(x_ref, gpre_ref, wgu_ref, wdown_ref, gpost_ref, o_ref):
    x = x_ref[0]
    xn = _rms(x, gpre_ref[...]).astype(BF16)
    hg = _dot(xn, wgu_ref[:, :D_FF])
    hu = _dot(xn, wgu_ref[:, D_FF:])
    act = ((hg / (1.0 + jnp.exp(-hg))) * hu).astype(BF16)
    f = _dot(act, wdown_ref[...])
    o_ref[0] = x + _rms(f, gpost_ref[...])


def _layer_spec(shape, layer):
    zeros = (0,) * len(shape)
    return pl.BlockSpec((None,) + tuple(shape), lambda b, t: (layer,) + zeros,
                        pipeline_mode=pl.Buffered(1))


def _const_spec(shape):
    zeros = (0,) * len(shape)
    return pl.BlockSpec(tuple(shape), lambda b, t: zeros, pipeline_mode=pl.Buffered(1))


def _tile_spec():
    return pl.BlockSpec((1, TM, D_MODEL), lambda b, t: (b, t, 0))


def _params(semantics):
    return pltpu.CompilerParams(dimension_semantics=semantics, vmem_limit_bytes=VMEM_LIMIT)


def _suffix_sum_matrix():
    r = np.arange(LANES)[:, None]
    s = np.arange(LANES)[None, :]
    suffix = (r > s) & (r // CHUNK == s // CHUNK)
    tot0 = np.broadcast_to(r // CHUNK == 0, (LANES, LANES))
    tot1 = np.broadcast_to(r // CHUNK == 1, (LANES, LANES))
    return np.concatenate([suffix, tot0, tot1], axis=1).astype(np.float32)


def kernel(x, mem, norm_mix_pre, w_in, w_gate_up, b_gate, gla_onorm, sg_ln_g, sg_ln_b, w_spatial,
           b_spatial, w_out, norm_mix_post, norm_x_pre, norm_mem, w_xq, w_xkv, w_xo, norm_x_post,
           norm_ffn_pre, w_ffn_gu, w_ffn_down, norm_ffn_post):
    batch, seq, _ = x.shape
    depth = w_in.shape[0]
    assert seq % TM == 0 and TM % SG_BLOCK == 0 and SG_BLOCK == 2 * CHUNK
    grid = (batch, seq // TM)

    def cols(a, b):
        return w_in[:, :, a:b]

    w_main = jnp.concatenate(
        [cols(OFF_Q, OFF_K), cols(OFF_V, OFF_G), cols(OFF_G, OFF_A), cols(OFF_U, OFF_SV),
         cols(OFF_SV, D_IN), cols(OFF_A, OFF_U),
         jnp.zeros((depth, D_MODEL, GATE_PAD - GATE_RANK), w_in.dtype)], axis=-1).astype(BF16)
    wk_t = jnp.swapaxes(cols(OFF_K, OFF_V), 1, 2).astype(BF16)
    wg_t = jnp.pad(jnp.swapaxes(w_gate_up, 1, 2),
                   ((0, 0), (0, 0), (0, GATE_PAD - GATE_RANK))).astype(BF16)
    bg = jnp.broadcast_to(b_gate[:, :, None], (depth, GLA_KW, LANES)).astype(F32)
    mm = jnp.asarray(_suffix_sum_matrix(), BF16)
    ws = w_spatial.astype(BF16)
    bsb = jnp.repeat(jnp.swapaxes(b_spatial, 1, 2), SG_GROUP_CH, axis=2).astype(F32)
    wout = w_out.astype(BF16)
    wq = w_xq.astype(BF16)
    wkv = w_xkv.astype(BF16)
    wo = w_xo.astype(BF16)
    wgu = w_ffn_gu.astype(BF16)
    wdown = w_ffn_down.astype(BF16)

    def vec(a):
        return a.reshape(depth, 1, a.shape[-1]).astype(F32)

    g_mix_pre, g_mix_post = vec(norm_mix_pre), vec(norm_mix_post)
    g_x_pre, g_x_post, g_mem = vec(norm_x_pre), vec(norm_x_post), vec(norm_mem)
    g_ffn_pre, g_ffn_post = vec(norm_ffn_pre), vec(norm_ffn_post)
    onorm, lng, lnb = vec(gla_onorm), vec(sg_ln_g), vec(sg_ln_b)

    x_shape = jax.ShapeDtypeStruct(x.shape, F32)
    kv_shape = jax.ShapeDtypeStruct((batch, N_MEM, D_MODEL), BF16)

    for l in range(depth):
        lspec = functools.partial(_layer_spec, layer=l)
        x = pl.pallas_call(
            _mixer_kernel,
            grid=grid,
            in_specs=[_tile_spec(), lspec((1, D_MODEL)), lspec((D_MODEL, D_MAIN)),
                      lspec((GLA_KW, D_MODEL)), lspec((GLA_KW, GATE_PAD)), lspec((GLA_KW, LANES)),
                      _const_spec((LANES, 3 * LANES)), lspec((1, GLA_WIDTH)), lspec((1, SG_WIDTH)),
                      lspec((1, SG_WIDTH)), lspec((SG_GROUPS, SG_BLOCK, SG_BLOCK)),
                      lspec((SG_BLOCK, SG_WIDTH)), lspec((D_MODEL, D_MODEL)), lspec((1, D_MODEL))],
            out_specs=_tile_spec(),
            out_shape=x_shape,
            scratch_shapes=[pltpu.VMEM((GLA_KW, GLA_DV), F32),
                            pltpu.VMEM((TM // CHUNK, GLA_KW, GLA_WIDTH), BF16),
                            pltpu.VMEM((TM, GLA_WIDTH), F32),
                            pltpu.VMEM((TM, SG_WIDTH), BF16)],
            compiler_params=_params(("arbitrary", "arbitrary")),
            name=f"mixer_l{l}",
        )(x, g_mix_pre, w_main, wk_t, wg_t, bg, mm, onorm, lng, lnb, ws, bsb, wout, g_mix_post)

        k_mem, v_mem = pl.pallas_call(
            _kv_kernel,
            grid=(batch,),
            in_specs=[pl.BlockSpec((1, N_MEM, D_MODEL), lambda b: (b, 0, 0)),
                      pl.BlockSpec((None, 1, D_MODEL), lambda b: (l, 0, 0)),
                      pl.BlockSpec((None, D_MODEL, 2 * D_MODEL), lambda b: (l, 0, 0))],
            out_specs=[pl.BlockSpec((1, N_MEM, D_MODEL), lambda b: (b, 0, 0)),
                       pl.BlockSpec((1, N_MEM, D_MODEL), lambda b: (b, 0, 0))],
            out_shape=[kv_shape, kv_shape],
            compiler_params=_params(("arbitrary",)),
            name=f"memkv_l{l}",
        )(mem, g_mem, wkv)

        x = pl.pallas_call(
            _attn_kernel,
            grid=grid,
            in_specs=[_tile_spec(), lspec((1, D_MODEL)), lspec((D_MODEL, D_MODEL)),
                      pl.BlockSpec((1, N_MEM, D_MODEL), lambda b, t: (b, 0, 0)),
                      pl.BlockSpec((1, N_MEM, D_MODEL), lambda b, t: (b, 0, 0)),
                      lspec((D_MODEL, D_MODEL)), lspec((1, D_MODEL))],
            out_specs=_tile_spec(),
            out_shape=x_shape,
            compiler_params=_params(("arbitrary", "arbitrary")),
            name=f"xattn_l{l}",
        )(x, g_x_pre, wq, k_mem, v_mem, wo, g_x_post)

        x = pl.pallas_call(
            _ffn_kernel,
            grid=grid,
            in_specs=[_tile_spec(), lspec((1, D_MODEL)), lspec((D_MODEL, 2 * D_FF)),
                      lspec((D_FF, D_MODEL)), lspec((1, D_MODEL))],
            out_specs=_tile_spec(),
            out_shape=x_shape,
            compiler_params=_params(("arbitrary", "arbitrary")),
            name=f"ffn_l{l}",
        )(x, g_ffn_pre, wgu, wdown, g_ffn_post)
    return x
```

```python
import functools

import numpy as np
import jax
import jax.numpy as jnp
from jax import lax
from jax.experimental import pallas as pl
from jax.experimental.pallas import tpu as pltpu

D_MODEL = 1024
CHUNK = 64
N_MEM = 256
EPS = 1e-6
GLA_HEADS = 4
GLA_DK = 64
GLA_DV = 128
GLA_WIDTH = GLA_HEADS * GLA_DV
GLA_KW = GLA_HEADS * GLA_DK
GATE_RANK = 16
GATE_TAU = 16.0
SG_WIDTH = D_MODEL - GLA_WIDTH
SG_GROUPS = 4
SG_GROUP_CH = SG_WIDTH // SG_GROUPS
SG_BLOCK = 128
X_HEADS = 4
X_HEAD_DIM = D_MODEL // X_HEADS
D_FF = 2816
OFF_Q = 0
OFF_K = OFF_Q + GLA_KW
OFF_V = OFF_K + GLA_KW
OFF_G = OFF_V + GLA_WIDTH
OFF_A = OFF_G + GLA_WIDTH
OFF_U = OFF_A + GATE_RANK
OFF_SV = OFF_U + SG_WIDTH
D_IN = OFF_SV + SG_WIDTH

LANES = 128
GATE_PAD = LANES
M_Q = 0
M_V = M_Q + GLA_KW
M_G = M_V + GLA_WIDTH
M_U = M_G + GLA_WIDTH
M_SV = M_U + SG_WIDTH
M_A = M_SV + SG_WIDTH
D_MAIN = M_A + GATE_PAD

TM = 1024
SUB = 256
VMEM_LIMIT = 56 * 1024 * 1024

F32 = jnp.float32
BF16 = jnp.bfloat16
NT_DIMS = (((1,), (1,)), ((), ()))


def _rms(x, g):
    ms = jnp.mean(x * x, axis=-1, keepdims=True)
    return x * lax.rsqrt(ms + EPS) * g


def _gelu_tanh(x):
    c = np.float32(np.sqrt(2.0 / np.pi))
    return 0.5 * x * (1.0 + jnp.tanh(c * (x + np.float32(0.044715) * (x * x * x))))


def _silu(x):
    return x / (1.0 + jnp.exp(-x))


def _dot(a, b):
    return jnp.dot(a, b, preferred_element_type=F32)


def _dot_nt(a, b):
    return lax.dot_general(a, b, NT_DIMS, preferred_element_type=F32)


def _mixer_subtile(x, state, chunk0, gpre_ref, wmain_ref, wkt_ref, wgt_ref, bg_ref, mm_ref,
                   onorm_ref, lng_ref, lnb_ref, ws_ref, bsb_ref, wout_ref, gpost_ref, sbd_ref):
    n_blocks = SUB // LANES
    xn = _rms(x, gpre_ref[...]).astype(BF16)
    h = _dot(xn, wmain_ref[...])
    k_t = _dot_nt(wkt_ref[...], xn)

    q = h[:, M_Q:M_Q + GLA_KW].astype(BF16)
    v = h[:, M_V:M_V + GLA_WIDTH].astype(BF16)
    a_lr = h[:, M_A:M_A + GATE_PAD].astype(BF16)

    z_t = _dot_nt(wgt_ref[...], a_lr) + jnp.concatenate([bg_ref[...]] * n_blocks, axis=1)
    la_t = (jnp.minimum(z_t, 0.0) - jnp.log(1.0 + jnp.exp(-jnp.abs(z_t)))) * np.float32(1.0 / GATE_TAU)
    la_hi = la_t.astype(BF16)
    la_lo = (la_t - la_hi.astype(F32)).astype(BF16)

    first_chunk = lax.broadcasted_iota(jnp.int32, (GLA_KW, LANES), 1) < CHUNK
    o_chunks = []
    for j in range(n_blocks):
        cl = slice(LANES * j, LANES * (j + 1))
        ex = jnp.exp(_dot(jnp.concatenate([la_hi[:, cl], la_lo[:, cl]], axis=1), mm_ref[...]))
        kd = k_t[:, cl] * ex[:, :LANES]
        kd_a = jnp.where(first_chunk, kd, 0.0).astype(BF16)
        kd_b = jnp.where(first_chunk, 0.0, kd).astype(BF16)
        incs = []
        for hh in range(GLA_HEADS):
            rs = slice(GLA_DK * hh, GLA_DK * (hh + 1))
            cs = slice(GLA_DV * hh, GLA_DV * (hh + 1))
            incs.append(_dot(jnp.concatenate([kd_a[rs], kd_b[rs]], axis=0), v[cl, cs]))
        for c2 in range(2):
            c = chunk0 + 2 * j + c2
            tot = ex[:, LANES * (1 + c2):LANES * (2 + c2)]
            for hh in range(GLA_HEADS):
                rs = slice(GLA_DK * hh, GLA_DK * (hh + 1))
                cs = slice(GLA_DV * hh, GLA_DV * (hh + 1))
                state[hh] = tot[rs] * state[hh] + incs[hh][GLA_DK * c2:GLA_DK * (c2 + 1)]
                sbd_ref[c, rs, cs] = state[hh].astype(BF16)
            rows = slice(LANES * j + CHUNK * c2, LANES * j + CHUNK * (c2 + 1))
            o_chunks.append(_dot(q[rows], sbd_ref[c]))

    o = jnp.concatenate(o_chunks, axis=0) * np.float32(GLA_DK ** -0.5)
    heads = []
    for hh in range(GLA_HEADS):
        oh = o[:, GLA_DV * hh:GLA_DV * (hh + 1)]
        heads.append(oh * lax.rsqrt(jnp.mean(oh * oh, axis=-1, keepdims=True) + EPS))
    o = jnp.concatenate(heads, axis=-1) * onorm_ref[...]
    gla_out = (o * _silu(h[:, M_G:M_G + GLA_WIDTH])).astype(BF16)

    u = _gelu_tanh(h[:, M_U:M_U + SG_WIDTH])
    sv = _gelu_tanh(h[:, M_SV:M_SV + SG_WIDTH])
    mu = jnp.mean(sv, axis=-1, keepdims=True)
    svc = sv - mu
    var = jnp.mean(svc * svc, axis=-1, keepdims=True)
    svn = (svc * lax.rsqrt(var + EPS) * lng_ref[...] + lnb_ref[...]).astype(BF16)
    row_chunk = lax.broadcasted_iota(jnp.int32, (SG_BLOCK, SG_BLOCK), 0) // CHUNK
    col_chunk = lax.broadcasted_iota(jnp.int32, (SG_BLOCK, SG_BLOCK), 1) // CHUNK
    causal = col_chunk <= row_chunk
    z_cols = []
    for gi in range(SG_GROUPS):
        cs = slice(SG_GROUP_CH * gi, SG_GROUP_CH * (gi + 1))
        w_g = jnp.where(causal, ws_ref[gi], jnp.zeros((), BF16))
        rhs = jnp.concatenate([svn[SG_BLOCK * j:SG_BLOCK * (j + 1), cs] for j in range(n_blocks)], axis=1)
        z = _dot(w_g, rhs)
        z_cols.append([z[:, SG_GROUP_CH * j:SG_GROUP_CH * (j + 1)] + bsb_ref[:, cs] for j in range(n_blocks)])
    z_full = jnp.concatenate(
        [jnp.concatenate([z_cols[gi][j] for gi in range(SG_GROUPS)], axis=1) for j in range(n_blocks)], axis=0)
    sg_out = (u * z_full).astype(BF16)

    mix = _dot(jnp.concatenate([gla_out, sg_out], axis=1), wout_ref[...])
    return x + _rms(mix, gpost_ref[...]), state


def _mixer_kernel(x_ref, gpre_ref, wmain_ref, wkt_ref, wgt_ref, bg_ref, mm_ref, onorm_ref,
                  lng_ref, lnb_ref, ws_ref, bsb_ref, wout_ref, gpost_ref, o_ref, s_ref, sbd_ref):
    @pl.when(pl.program_id(1) == 0)
    def _():
        s_ref[...] = jnp.zeros_like(s_ref)
        sbd_ref[...] = jnp.zeros_like(sbd_ref)

    state = [s_ref[GLA_DK * hh:GLA_DK * (hh + 1), :] for hh in range(GLA_HEADS)]
    for s in range(TM // SUB):
        r = slice(SUB * s, SUB * (s + 1))
        out, state = _mixer_subtile(
            x_ref[0, r, :], state, s * (SUB // CHUNK), gpre_ref, wmain_ref, wkt_ref, wgt_ref, bg_ref,
            mm_ref, onorm_ref, lng_ref, lnb_ref, ws_ref, bsb_ref, wout_ref, gpost_ref, sbd_ref)
        o_ref[0, r, :] = out
    for hh in range(GLA_HEADS):
        s_ref[GLA_DK * hh:GLA_DK * (hh + 1), :] = state[hh]


def _kv_kernel(mem_ref, gmem_ref, wkv_ref, k_ref, v_ref):
    memn = _rms(mem_ref[0], gmem_ref[...]).astype(BF16)
    kv = _dot(memn, wkv_ref[...])
    k_ref[0] = kv[:, :D_MODEL].astype(BF16)
    v_ref[0] = kv[:, D_MODEL:].astype(BF16)


def _attn_kernel(x_ref, gpre_ref, wq_ref, k_ref, v_ref, wo_ref, gpost_ref, o_ref):
    for s in range(TM // SUB):
        r = slice(SUB * s, SUB * (s + 1))
        x = x_ref[0, r, :]
        xn = _rms(x, gpre_ref[...]).astype(BF16)
        q = _dot(xn, wq_ref[...]).astype(BF16)
        outs = []
        for hh in range(X_HEADS):
            cs = slice(X_HEAD_DIM * hh, X_HEAD_DIM * (hh + 1))
            sc = _dot_nt(q[:, cs], k_ref[0, :, cs]) * np.float32(X_HEAD_DIM ** -0.5)
            p = jnp.exp(sc - jnp.max(sc, axis=-1, keepdims=True))
            denom = jnp.sum(p, axis=-1, keepdims=True)
            outs.append((_dot(p.astype(BF16), v_ref[0, :, cs]) / denom).astype(BF16))
        xa = _dot(jnp.concatenate(outs, axis=-1), wo_ref[...])
        o_ref[0, r, :] = x + _rms(xa, gpost_ref[...])


def _ffn_kernel(x_ref, gpre_ref, wgu_ref, wdown_ref, gpost_ref, o_ref):
    for s in range(TM // SUB):
        r = slice(SUB * s, SUB * (s + 1))
        x = x_ref[0, r, :]
        xn = _rms(x, gpre_ref[...]).astype(BF16)
        hg = _dot(xn, wgu_ref[:, :D_FF])
        hu = _dot(xn, wgu_ref[:, D_FF:])
        act = (_silu(hg) * hu).astype(BF16)
        f = _dot(act, wdown_ref[...])
        o_ref[0, r, :] = x + _rms(f, gpost_ref[...])


def _layer_spec(shape, layer):
    zeros = (0,) * len(shape)
    return pl.BlockSpec((None,) + tuple(shape), lambda b, t: (layer,) + zeros,
                        pipeline_mode=pl.Buffered(1))


def _const_spec(shape):
    zeros = (0,) * len(shape)
    return pl.BlockSpec(tuple(shape), lambda b, t: zeros, pipeline_mode=pl.Buffered(1))


def _tile_spec():
    return pl.BlockSpec((1, TM, D_MODEL), lambda b, t: (b, t, 0))


def _params(semantics):
    return pltpu.CompilerParams(dimension_semantics=semantics, vmem_limit_bytes=VMEM_LIMIT)


def _suffix_sum_matrix():
    r = np.arange(LANES)[:, None]
    s = np.arange(LANES)[None, :]
    suffix = (r > s) & (r // CHUNK == s // CHUNK)
    tot0 = np.broadcast_to(r // CHUNK == 0, (LANES, LANES))
    tot1 = np.broadcast_to(r // CHUNK == 1, (LANES, LANES))
    m = np.concatenate([suffix, tot0, tot1], axis=1).astype(np.float32)
    return np.concatenate([m, m], axis=0)


def kernel(x, mem, norm_mix_pre, w_in, w_gate_up, b_gate, gla_onorm, sg_ln_g, sg_ln_b, w_spatial,
           b_spatial, w_out, norm_mix_post, norm_x_pre, norm_mem, w_xq, w_xkv, w_xo, norm_x_post,
           norm_ffn_pre, w_ffn_gu, w_ffn_down, norm_ffn_post):
    batch, seq, _ = x.shape
    depth = w_in.shape[0]
    assert seq % TM == 0 and TM % SUB == 0 and SUB % SG_BLOCK == 0 and SG_BLOCK == 2 * CHUNK
    grid = (batch, seq // TM)

    def cols(a, b):
        return w_in[:, :, a:b]

    w_main = jnp.concatenate(
        [cols(OFF_Q, OFF_K), cols(OFF_V, OFF_G), cols(OFF_G, OFF_A), cols(OFF_U, OFF_SV),
         cols(OFF_SV, D_IN), cols(OFF_A, OFF_U),
         jnp.zeros((depth, D_MODEL, GATE_PAD - GATE_RANK), w_in.dtype)], axis=-1).astype(BF16)
    wk_t = jnp.swapaxes(cols(OFF_K, OFF_V), 1, 2).astype(BF16)
    wg_t = jnp.pad(jnp.swapaxes(w_gate_up, 1, 2),
                   ((0, 0), (0, 0), (0, GATE_PAD - GATE_RANK))).astype(BF16)
    bg = jnp.broadcast_to(b_gate[:, :, None], (depth, GLA_KW, LANES)).astype(F32)
    mm = jnp.asarray(_suffix_sum_matrix(), BF16)
    ws = w_spatial.astype(BF16)
    bsb = jnp.repeat(jnp.swapaxes(b_spatial, 1, 2), SG_GROUP_CH, axis=2).astype(F32)
    wout = w_out.astype(BF16)
    wq = w_xq.astype(BF16)
    wkv = w_xkv.astype(BF16)
    wo = w_xo.astype(BF16)
    wgu = w_ffn_gu.astype(BF16)
    wdown = w_ffn_down.astype(BF16)

    def vec(a):
        return a.reshape(depth, 1, a.shape[-1]).astype(F32)

    g_mix_pre, g_mix_post = vec(norm_mix_pre), vec(norm_mix_post)
    g_x_pre, g_x_post, g_mem = vec(norm_x_pre), vec(norm_x_post), vec(norm_mem)
    g_ffn_pre, g_ffn_post = vec(norm_ffn_pre), vec(norm_ffn_post)
    onorm, lng, lnb = vec(gla_onorm), vec(sg_ln_g), vec(sg_ln_b)

    x_shape = jax.ShapeDtypeStruct(x.shape, F32)
    kv_shape = jax.ShapeDtypeStruct((batch, N_MEM, D_MODEL), BF16)

    for l in range(depth):
        lspec = functools.partial(_layer_spec, layer=l)
        x = pl.pallas_call(
            _mixer_kernel,
            grid=grid,
            in_specs=[_tile_spec(), lspec((1, D_MODEL)), lspec((D_MODEL, D_MAIN)),
                      lspec((GLA_KW, D_MODEL)), lspec((GLA_KW, GATE_PAD)), lspec((GLA_KW, LANES)),
                      _const_spec((2 * LANES, 3 * LANES)), lspec((1, GLA_WIDTH)), lspec((1, SG_WIDTH)),
                      lspec((1, SG_WIDTH)), lspec((SG_GROUPS, SG_BLOCK, SG_BLOCK)),
                      lspec((SG_BLOCK, SG_WIDTH)), lspec((D_MODEL, D_MODEL)), lspec((1, D_MODEL))],
            out_specs=_tile_spec(),
            out_shape=x_shape,
            scratch_shapes=[pltpu.VMEM((GLA_KW, GLA_DV), F32),
                            pltpu.VMEM((TM // CHUNK, GLA_KW, GLA_WIDTH), BF16)],
            compiler_params=_params(("arbitrary", "arbitrary")),
            name=f"mixer_l{l}",
        )(x, g_mix_pre, w_main, wk_t, wg_t, bg, mm, onorm, lng, lnb, ws, bsb, wout, g_mix_post)

        k_mem, v_mem = pl.pallas_call(
            _kv_kernel,
            grid=(batch,),
            in_specs=[pl.BlockSpec((1, N_MEM, D_MODEL), lambda b: (b, 0, 0)),
                      pl.BlockSpec((None, 1, D_MODEL), lambda b: (l, 0, 0)),
                      pl.BlockSpec((None, D_MODEL, 2 * D_MODEL), lambda b: (l, 0, 0))],
            out_specs=[pl.BlockSpec((1, N_MEM, D_MODEL), lambda b: (b, 0, 0)),
                       pl.BlockSpec((1, N_MEM, D_MODEL), lambda b: (b, 0, 0))],
            out_shape=[kv_shape, kv_shape],
            compiler_params=_params(("arbitrary",)),
            name=f"memkv_l{l}",
        )(mem, g_mem, wkv)

        x = pl.pallas_call(
            _attn_kernel,
            grid=grid,
            in_specs=[_tile_spec(), lspec((1, D_MODEL)), lspec((D_MODEL, D_MODEL)),
                      pl.BlockSpec((1, N_MEM, D_MODEL), lambda b, t: (b, 0, 0)),
                      pl.BlockSpec((1, N_MEM, D_MODEL), lambda b, t: (b, 0, 0)),
                      lspec((D_MODEL, D_MODEL)), lspec((1, D_MODEL))],
            out_specs=_tile_spec(),
            out_shape=x_shape,
            compiler_params=_params(("arbitrary", "arbitrary")),
            name=f"xattn_l{l}",
        )(x, g_x_pre, wq, k_mem, v_mem, wo, g_x_post)

        x = pl.pallas_call(
            _ffn_kernel,
            grid=grid,
            in_specs=[_tile_spec(), lspec((1, D_MODEL)), lspec((D_MODEL, 2 * D_FF)),
                      lspec((D_FF, D_MODEL)), lspec((1, D_MODEL))],
            out_specs=_tile_spec(),
            out_shape=x_shape,
            compiler_params=_params(("arbitrary", "arbitrary")),
            name=f"ffn_l{l}",
        )(x, g_ffn_pre, wgu, wdown, g_ffn_post)
    return x
```

```python
import functools

import numpy as np
import jax
import jax.numpy as jnp
from jax import lax
from jax.experimental import pallas as pl
from jax.experimental.pallas import tpu as pltpu

D_MODEL = 1024
CHUNK = 64
N_MEM = 256
EPS = 1e-6
GLA_HEADS = 4
GLA_DK = 64
GLA_DV = 128
GLA_WIDTH = GLA_HEADS * GLA_DV
GLA_KW = GLA_HEADS * GLA_DK
GATE_RANK = 16
GATE_TAU = 16.0
SG_WIDTH = D_MODEL - GLA_WIDTH
SG_GROUPS = 4
SG_GROUP_CH = SG_WIDTH // SG_GROUPS
SG_BLOCK = 128
X_HEADS = 4
X_HEAD_DIM = D_MODEL // X_HEADS
D_FF = 2816
OFF_Q = 0
OFF_K = OFF_Q + GLA_KW
OFF_V = OFF_K + GLA_KW
OFF_G = OFF_V + GLA_WIDTH
OFF_A = OFF_G + GLA_WIDTH
OFF_U = OFF_A + GATE_RANK
OFF_SV = OFF_U + SG_WIDTH
D_IN = OFF_SV + SG_WIDTH

LANES = 128
GATE_PAD = LANES
M_Q = 0
M_A = M_Q + GLA_KW
M_V = M_A + GATE_PAD
M_G = M_V + GLA_WIDTH
M_U = M_G + GLA_WIDTH
M_SV = M_U + SG_WIDTH
D_MAIN = M_SV + SG_WIDTH

TM = 1024
SUB = 256
VMEM_LIMIT = 56 * 1024 * 1024

F32 = jnp.float32
BF16 = jnp.bfloat16
NT_DIMS = (((1,), (1,)), ((), ()))


def _rms(x, g):
    ms = jnp.mean(x * x, axis=-1, keepdims=True)
    return x * lax.rsqrt(ms + EPS) * g


def _gelu_tanh(x):
    c = np.float32(np.sqrt(2.0 / np.pi))
    return 0.5 * x * (1.0 + jnp.tanh(c * (x + np.float32(0.044715) * (x * x * x))))


def _silu(x):
    return x / (1.0 + jnp.exp(-x))


def _dot(a, b):
    return jnp.dot(a, b, preferred_element_type=F32)


def _dot_nt(a, b):
    return lax.dot_general(a, b, NT_DIMS, preferred_element_type=F32)


def _mixer_kernel(x_ref, gpre_ref, wmain_ref, wkt_ref, wgt_ref, bg_ref, mm_ref, onorm_ref,
                  lng_ref, lnb_ref, ws_ref, bsb_ref, wout_ref, gpost_ref, o_ref, s_ref, sbd_ref):
    n_sub = TM // SUB
    n_blocks = SUB // LANES

    @pl.when(pl.program_id(1) == 0)
    def _():
        s_ref[...] = jnp.zeros_like(s_ref)
        sbd_ref[...] = jnp.zeros_like(sbd_ref)

    state = [s_ref[GLA_DK * hh:GLA_DK * (hh + 1), :] for hh in range(GLA_HEADS)]
    ctx = [dict() for _ in range(n_sub)]
    first_chunk = lax.broadcasted_iota(jnp.int32, (GLA_KW, LANES), 1) < CHUNK
    row_chunk = lax.broadcasted_iota(jnp.int32, (SG_BLOCK, SG_BLOCK), 0) // CHUNK
    col_chunk = lax.broadcasted_iota(jnp.int32, (SG_BLOCK, SG_BLOCK), 1) // CHUNK
    causal = col_chunk <= row_chunk

    def pre_norm(k):
        c = ctx[k]
        c["x"] = x_ref[0, SUB * k:SUB * (k + 1), :]
        c["xn"] = _rms(c["x"], gpre_ref[...]).astype(BF16)

    def proj_qa(k):
        c = ctx[k]
        h = _dot(c["xn"], wmain_ref[:, M_Q:M_V])
        c["q"] = h[:, M_Q:M_Q + GLA_KW].astype(BF16)
        c["a_lr"] = h[:, M_A:M_A + GATE_PAD].astype(BF16)

    def proj_k(k):
        c = ctx[k]
        c["k_t"] = _dot_nt(wkt_ref[...], c["xn"])

    def proj_v(k):
        c = ctx[k]
        c["v"] = _dot(c["xn"], wmain_ref[:, M_V:M_G]).astype(BF16)

    def proj_sv(k):
        c = ctx[k]
        sv = _gelu_tanh(_dot(c["xn"], wmain_ref[:, M_SV:D_MAIN]))
        mu = jnp.mean(sv, axis=-1, keepdims=True)
        svc = sv - mu
        var = jnp.mean(svc * svc, axis=-1, keepdims=True)
        c["svn"] = (svc * lax.rsqrt(var + EPS) * lng_ref[...] + lnb_ref[...]).astype(BF16)

    def proj_g(k):
        c = ctx[k]
        c["gate"] = _silu(_dot(c["xn"], wmain_ref[:, M_G:M_U]))

    def proj_u(k):
        c = ctx[k]
        c["u"] = _gelu_tanh(_dot(c["xn"], wmain_ref[:, M_U:M_SV]))

    def gate_logits(k):
        c = ctx[k]
        z_t = _dot_nt(wgt_ref[...], c["a_lr"]) + jnp.concatenate([bg_ref[...]] * n_blocks, axis=1)
        la_t = (jnp.minimum(z_t, 0.0) - jnp.log(1.0 + jnp.exp(-jnp.abs(z_t)))) * np.float32(1.0 / GATE_TAU)
        c["la_hi"] = la_t.astype(BF16)
        c["la_lo"] = (la_t - c["la_hi"].astype(F32)).astype(BF16)

    def decays(k):
        c = ctx[k]
        c["ex"], c["kd_a"], c["kd_b"] = [], [], []
        for j in range(n_blocks):
            cl = slice(LANES * j, LANES * (j + 1))
            ex = jnp.exp(_dot(jnp.concatenate([c["la_hi"][:, cl], c["la_lo"][:, cl]], axis=1), mm_ref[...]))
            kd = c["k_t"][:, cl] * ex[:, :LANES]
            c["ex"].append(ex)
            c["kd_a"].append(jnp.where(first_chunk, kd, 0.0).astype(BF16))
            c["kd_b"].append(jnp.where(first_chunk, 0.0, kd).astype(BF16))

    def increments(k):
        c = ctx[k]
        c["incs"] = []
        for j in range(n_blocks):
            cl = slice(LANES * j, LANES * (j + 1))
            incs = []
            for hh in range(GLA_HEADS):
                rs = slice(GLA_DK * hh, GLA_DK * (hh + 1))
                cs = slice(GLA_DV * hh, GLA_DV * (hh + 1))
                incs.append(_dot(jnp.concatenate([c["kd_a"][j][rs], c["kd_b"][j][rs]], axis=0), c["v"][cl, cs]))
            c["incs"].append(incs)

    def spatial_mix(k):
        c = ctx[k]
        z_cols = []
        for gi in range(SG_GROUPS):
            cs = slice(SG_GROUP_CH * gi, SG_GROUP_CH * (gi + 1))
            w_g = jnp.where(causal, ws_ref[gi], jnp.zeros((), BF16))
            rhs = jnp.concatenate([c["svn"][SG_BLOCK * j:SG_BLOCK * (j + 1), cs] for j in range(n_blocks)], axis=1)
            z = _dot(w_g, rhs)
            z_cols.append([z[:, SG_GROUP_CH * j:SG_GROUP_CH * (j + 1)] + bsb_ref[:, cs] for j in range(n_blocks)])
        z_full = jnp.concatenate(
            [jnp.concatenate([z_cols[gi][j] for gi in range(SG_GROUPS)], axis=1) for j in range(n_blocks)], axis=0)
        c["sg_out"] = (c["u"] * z_full).astype(BF16)

    def recurrence_readout(k):
        c = ctx[k]
        o_chunks = []
        for j in range(n_blocks):
            for c2 in range(2):
                ci = k * (SUB // CHUNK) + 2 * j + c2
                tot = c["ex"][j][:, LANES * (1 + c2):LANES * (2 + c2)]
                for hh in range(GLA_HEADS):
                    rs = slice(GLA_DK * hh, GLA_DK * (hh + 1))
                    cs = slice(GLA_DV * hh, GLA_DV * (hh + 1))
                    state[hh] = tot[rs] * state[hh] + c["incs"][j][hh][GLA_DK * c2:GLA_DK * (c2 + 1)]
                    sbd_ref[ci, rs, cs] = state[hh].astype(BF16)
                rows = slice(LANES * j + CHUNK * c2, LANES * j + CHUNK * (c2 + 1))
                o_chunks.append(_dot(c["q"][rows], sbd_ref[ci]))
        c["o"] = jnp.concatenate(o_chunks, axis=0)

    def out_proj(k):
        c = ctx[k]
        heads = []
        for hh in range(GLA_HEADS):
            oh = c["o"][:, GLA_DV * hh:GLA_DV * (hh + 1)]
            heads.append(oh * lax.rsqrt(jnp.mean(oh * oh, axis=-1, keepdims=True) + EPS))
        gla_out = (jnp.concatenate(heads, axis=-1) * onorm_ref[...] * c["gate"]).astype(BF16)
        mix = _dot(jnp.concatenate([gla_out, c["sg_out"]], axis=1), wout_ref[...])
        o_ref[0, SUB * k:SUB * (k + 1), :] = c["x"] + _rms(mix, gpost_ref[...])
        c.clear()

    pre_norm(0)
    proj_qa(0)
    proj_k(0)
    proj_v(0)
    proj_sv(0)
    proj_g(0)
    proj_u(0)
    for k in range(n_sub):
        nxt = k + 1 < n_sub
        gate_logits(k)
        if nxt:
            pre_norm(k + 1)
            proj_qa(k + 1)
            proj_k(k + 1)
        decays(k)
        if nxt:
            proj_v(k + 1)
        increments(k)
        if nxt:
            proj_sv(k + 1)
        spatial_mix(k)
        recurrence_readout(k)
        if nxt:
            proj_g(k + 1)
            proj_u(k + 1)
        out_proj(k)

    for hh in range(GLA_HEADS):
        s_ref[GLA_DK * hh:GLA_DK * (hh + 1), :] = state[hh]


def _kv_kernel(mem_ref, gmem_ref, wkv_ref, k_ref, v_ref):
    memn = _rms(mem_ref[0], gmem_ref[...]).astype(BF16)
    kv = _dot(memn, wkv_ref[...])
    k_ref[0] = kv[:, :D_MODEL].astype(BF16)
    v_ref[0] = kv[:, D_MODEL:].astype(BF16)


def _attn_kernel(x_ref, gpre_ref, wq_ref, k_ref, v_ref, wo_ref, gpost_ref, o_ref):
    n_sub = TM // SUB
    ctx = [dict() for _ in range(n_sub)]

    def q_proj(k):
        c = ctx[k]
        c["x"] = x_ref[0, SUB * k:SUB * (k + 1), :]
        xn = _rms(c["x"], gpre_ref[...]).astype(BF16)
        c["q"] = _dot(xn, wq_ref[...]).astype(BF16)

    def scores(k):
        c = ctx[k]
        c["p"], c["denom"] = [], []
        for hh in range(X_HEADS):
            cs = slice(X_HEAD_DIM * hh, X_HEAD_DIM * (hh + 1))
            sc = _dot_nt(c["q"][:, cs], k_ref[0, :, cs])
            p = jnp.exp(sc - jnp.max(sc, axis=-1, keepdims=True))
            c["denom"].append(jnp.sum(p, axis=-1, keepdims=True))
            c["p"].append(p.astype(BF16))

    def values(k):
        c = ctx[k]
        outs = []
        for hh in range(X_HEADS):
            cs = slice(X_HEAD_DIM * hh, X_HEAD_DIM * (hh + 1))
            outs.append((_dot(c["p"][hh], v_ref[0, :, cs]) / c["denom"][hh]).astype(BF16))
        c["o"] = jnp.concatenate(outs, axis=-1)

    def out_proj(k):
        c = ctx[k]
        xa = _dot(c["o"], wo_ref[...])
        o_ref[0, SUB * k:SUB * (k + 1), :] = c["x"] + _rms(xa, gpost_ref[...])
        c.clear()

    q_proj(0)
    scores(0)
    for k in range(n_sub):
        nxt = k + 1 < n_sub
        if nxt:
            q_proj(k + 1)
        values(k)
        if nxt:
            scores(k + 1)
        out_proj(k)


def _ffn_kernel(x_ref, gpre_ref, wgu_ref, wdown_ref, gpost_ref, o_ref):
    for s in range(TM // SUB):
        r = slice(SUB * s, SUB * (s + 1))
        x = x_ref[0, r, :]
        xn = _rms(x, gpre_ref[...]).astype(BF16)
        hg = _dot(xn, wgu_ref[:, :D_FF])
        hu = _dot(xn, wgu_ref[:, D_FF:])
        act = (_silu(hg) * hu).astype(BF16)
        f = _dot(act, wdown_ref[...])
        o_ref[0, r, :] = x + _rms(f, gpost_ref[...])


def _layer_spec(shape, layer):
    zeros = (0,) * len(shape)
    return pl.BlockSpec((None,) + tuple(shape), lambda b, t: (layer,) + zeros,
                        pipeline_mode=pl.Buffered(1))


def _const_spec(shape):
    zeros = (0,) * len(shape)
    return pl.BlockSpec(tuple(shape), lambda b, t: zeros, pipeline_mode=pl.Buffered(1))


def _tile_spec():
    return pl.BlockSpec((1, TM, D_MODEL), lambda b, t: (b, t, 0))


def _params(semantics):
    return pltpu.CompilerParams(dimension_semantics=semantics, vmem_limit_bytes=VMEM_LIMIT)


def _suffix_sum_matrix():
    r = np.arange(LANES)[:, None]
    s = np.arange(LANES)[None, :]
    suffix = (r > s) & (r // CHUNK == s // CHUNK)
    tot0 = np.broadcast_to(r // CHUNK == 0, (LANES, LANES))
    tot1 = np.broadcast_to(r // CHUNK == 1, (LANES, LANES))
    m = np.concatenate([suffix, tot0, tot1], axis=1).astype(np.float32)
    return np.concatenate([m, m], axis=0)


def kernel(x, mem, norm_mix_pre, w_in, w_gate_up, b_gate, gla_onorm, sg_ln_g, sg_ln_b, w_spatial,
           b_spatial, w_out, norm_mix_post, norm_x_pre, norm_mem, w_xq, w_xkv, w_xo, norm_x_post,
           norm_ffn_pre, w_ffn_gu, w_ffn_down, norm_ffn_post):
    batch, seq, _ = x.shape
    depth = w_in.shape[0]
    assert seq % TM == 0 and TM % SUB == 0 and SUB % SG_BLOCK == 0 and SG_BLOCK == 2 * CHUNK
    grid = (batch, seq // TM)

    def cols(a, b):
        return w_in[:, :, a:b]

    w_main = jnp.concatenate(
        [cols(OFF_Q, OFF_K) * np.float32(GLA_DK ** -0.5), cols(OFF_A, OFF_U),
         jnp.zeros((depth, D_MODEL, GATE_PAD - GATE_RANK), w_in.dtype),
         cols(OFF_V, OFF_G), cols(OFF_G, OFF_A), cols(OFF_U, OFF_SV), cols(OFF_SV, D_IN)],
        axis=-1).astype(BF16)
    wk_t = jnp.swapaxes(cols(OFF_K, OFF_V), 1, 2).astype(BF16)
    wg_t = jnp.pad(jnp.swapaxes(w_gate_up, 1, 2),
                   ((0, 0), (0, 0), (0, GATE_PAD - GATE_RANK))).astype(BF16)
    bg = jnp.broadcast_to(b_gate[:, :, None], (depth, GLA_KW, LANES)).astype(F32)
    mm = jnp.asarray(_suffix_sum_matrix(), BF16)
    ws = w_spatial.astype(BF16)
    bsb = jnp.repeat(jnp.swapaxes(b_spatial, 1, 2), SG_GROUP_CH, axis=2).astype(F32)
    wout = w_out.astype(BF16)
    wq = (w_xq * np.float32(X_HEAD_DIM ** -0.5)).astype(BF16)
    wkv = w_xkv.astype(BF16)
    wo = w_xo.astype(BF16)
    wgu = w_ffn_gu.astype(BF16)
    wdown = w_ffn_down.astype(BF16)

    def vec(a):
        return a.reshape(depth, 1, a.shape[-1]).astype(F32)

    g_mix_pre, g_mix_post = vec(norm_mix_pre), vec(norm_mix_post)
    g_x_pre, g_x_post, g_mem = vec(norm_x_pre), vec(norm_x_post), vec(norm_mem)
    g_ffn_pre, g_ffn_post = vec(norm_ffn_pre), vec(norm_ffn_post)
    onorm, lng, lnb = vec(gla_onorm), vec(sg_ln_g), vec(sg_ln_b)

    x_shape = jax.ShapeDtypeStruct(x.shape, F32)
    kv_shape = jax.ShapeDtypeStruct((batch, N_MEM, D_MODEL), BF16)

    for l in range(depth):
        lspec = functools.partial(_layer_spec, layer=l)
        x = pl.pallas_call(
            _mixer_kernel,
            grid=grid,
            in_specs=[_tile_spec(), lspec((1, D_MODEL)), lspec((D_MODEL, D_MAIN)),
                      lspec((GLA_KW, D_MODEL)), lspec((GLA_KW, GATE_PAD)), lspec((GLA_KW, LANES)),
                      _const_spec((2 * LANES, 3 * LANES)), lspec((1, GLA_WIDTH)), lspec((1, SG_WIDTH)),
                      lspec((1, SG_WIDTH)), lspec((SG_GROUPS, SG_BLOCK, SG_BLOCK)),
                      lspec((SG_BLOCK, SG_WIDTH)), lspec((D_MODEL, D_MODEL)), lspec((1, D_MODEL))],
            out_specs=_tile_spec(),
            out_shape=x_shape,
            scratch_shapes=[pltpu.VMEM((GLA_KW, GLA_DV), F32),
                            pltpu.VMEM((TM // CHUNK, GLA_KW, GLA_WIDTH), BF16)],
            compiler_params=_params(("arbitrary", "arbitrary")),
            name=f"mixer_l{l}",
        )(x, g_mix_pre, w_main, wk_t, wg_t, bg, mm, onorm, lng, lnb, ws, bsb, wout, g_mix_post)

        k_mem, v_mem = pl.pallas_call(
            _kv_kernel,
            grid=(batch,),
            in_specs=[pl.BlockSpec((1, N_MEM, D_MODEL), lambda b: (b, 0, 0)),
                      pl.BlockSpec((None, 1, D_MODEL), lambda b: (l, 0, 0)),
                      pl.BlockSpec((None, D_MODEL, 2 * D_MODEL), lambda b: (l, 0, 0))],
            out_specs=[pl.BlockSpec((1, N_MEM, D_MODEL), lambda b: (b, 0, 0)),
                       pl.BlockSpec((1, N_MEM, D_MODEL), lambda b: (b, 0, 0))],
            out_shape=[kv_shape, kv_shape],
            compiler_params=_params(("arbitrary",)),
            name=f"memkv_l{l}",
        )(mem, g_mem, wkv)

        x = pl.pallas_call(
            _attn_kernel,
            grid=grid,
            in_specs=[_tile_spec(), lspec((1, D_MODEL)), lspec((D_MODEL, D_MODEL)),
                      pl.BlockSpec((1, N_MEM, D_MODEL), lambda b, t: (b, 0, 0)),
                      pl.BlockSpec((1, N_MEM, D_MODEL), lambda b, t: (b, 0, 0)),
                      lspec((D_MODEL, D_MODEL)), lspec((1, D_MODEL))],
            out_specs=_tile_spec(),
            out_shape=x_shape,
            compiler_params=_params(("arbitrary", "arbitrary")),
            name=f"xattn_l{l}",
        )(x, g_x_pre, wq, k_mem, v_mem, wo, g_x_post)

        x = pl.pallas_call(
            _ffn_kernel,
            grid=grid,
            in_specs=[_tile_spec(), lspec((1, D_MODEL)), lspec((D_MODEL, 2 * D_FF)),
                      lspec((D_FF, D_MODEL)), lspec((1, D_MODEL))],
            out_specs=_tile_spec(),
            out_shape=x_shape,
            compiler_params=_params(("arbitrary", "arbitrary")),
            name=f"ffn_l{l}",
        )(x, g_ffn_pre, wgu, wdown, g_ffn_post)
    return x
```

```python
import functools

import numpy as np
import jax
import jax.numpy as jnp
from jax import lax
from jax.experimental import pallas as pl
from jax.experimental.pallas import tpu as pltpu

D_MODEL = 1024
CHUNK = 64
N_MEM = 256
EPS = 1e-6
GLA_HEADS = 4
GLA_DK = 64
GLA_DV = 128
GLA_WIDTH = GLA_HEADS * GLA_DV
GLA_KW = GLA_HEADS * GLA_DK
GATE_RANK = 16
GATE_TAU = 16.0
SG_WIDTH = D_MODEL - GLA_WIDTH
SG_GROUPS = 4
SG_GROUP_CH = SG_WIDTH // SG_GROUPS
SG_BLOCK = 128
X_HEADS = 4
X_HEAD_DIM = D_MODEL // X_HEADS
D_FF = 2816
OFF_Q = 0
OFF_K = OFF_Q + GLA_KW
OFF_V = OFF_K + GLA_KW
OFF_G = OFF_V + GLA_WIDTH
OFF_A = OFF_G + GLA_WIDTH
OFF_U = OFF_A + GATE_RANK
OFF_SV = OFF_U + SG_WIDTH
D_IN = OFF_SV + SG_WIDTH

LANES = 128
GATE_PAD = LANES
M_Q = 0
M_A = M_Q + GLA_KW
M_K = M_A + GATE_PAD
M_V = M_K + GLA_KW
M_G = M_V + GLA_WIDTH
M_U = M_G + GLA_WIDTH
M_SV = M_U + SG_WIDTH
D_MAIN = M_SV + SG_WIDTH

TM = 1024
SUB = 256
VMEM_LIMIT = 56 * 1024 * 1024

F32 = jnp.float32
BF16 = jnp.bfloat16
NT_DIMS = (((1,), (1,)), ((), ()))


def _rms(x, g):
    ms = jnp.mean(x * x, axis=-1, keepdims=True)
    return x * lax.rsqrt(ms + EPS) * g


def _gelu_tanh(x):
    c = np.float32(np.sqrt(2.0 / np.pi))
    return 0.5 * x * (1.0 + jnp.tanh(c * (x + np.float32(0.044715) * (x * x * x))))


def _silu(x):
    return x / (1.0 + jnp.exp(-x))


def _dot(a, b):
    return jnp.dot(a, b, preferred_element_type=F32)


def _dot_nt(a, b):
    return lax.dot_general(a, b, NT_DIMS, preferred_element_type=F32)


def _mixer_kernel(x_ref, gpre_ref, wmain_ref, wgt_ref, bg_ref, mm_ref, onorm_ref,
                  lng_ref, lnb_ref, ws_ref, bsb_ref, wout_ref, gpost_ref, o_ref, s_ref, sbd_ref):
    n_sub = TM // SUB
    n_blocks = SUB // LANES

    @pl.when(pl.program_id(1) == 0)
    def _():
        s_ref[...] = jnp.zeros_like(s_ref)
        sbd_ref[...] = jnp.zeros_like(sbd_ref)

    state = [s_ref[GLA_DK * hh:GLA_DK * (hh + 1), :] for hh in range(GLA_HEADS)]
    ctx = [dict() for _ in range(n_sub)]
    row_chunk = lax.broadcasted_iota(jnp.int32, (SG_BLOCK, SG_BLOCK), 0) // CHUNK
    col_chunk = lax.broadcasted_iota(jnp.int32, (SG_BLOCK, SG_BLOCK), 1) // CHUNK
    causal = col_chunk <= row_chunk

    def pre_norm(k):
        c = ctx[k]
        c["x"] = x_ref[0, SUB * k:SUB * (k + 1), :]
        c["xn"] = _rms(c["x"], gpre_ref[...]).astype(BF16)

    def proj_qak(k):
        c = ctx[k]
        h = _dot(c["xn"], wmain_ref[:, M_Q:M_V])
        c["q"] = h[:, M_Q:M_Q + GLA_KW].astype(BF16)
        c["a_lr"] = h[:, M_A:M_A + GATE_PAD].astype(BF16)
        c["k_t"] = h[:, M_K:M_K + GLA_KW].T

    def proj_v(k):
        c = ctx[k]
        c["v"] = _dot(c["xn"], wmain_ref[:, M_V:M_G]).astype(BF16)

    def proj_sv(k):
        c = ctx[k]
        sv = _gelu_tanh(_dot(c["xn"], wmain_ref[:, M_SV:D_MAIN]))
        mu = jnp.mean(sv, axis=-1, keepdims=True)
        svc = sv - mu
        var = jnp.mean(svc * svc, axis=-1, keepdims=True)
        c["svn"] = (svc * lax.rsqrt(var + EPS) * lng_ref[...] + lnb_ref[...]).astype(BF16)

    def proj_g(k):
        c = ctx[k]
        c["gate"] = _silu(_dot(c["xn"], wmain_ref[:, M_G:M_U]))

    def proj_u(k):
        c = ctx[k]
        c["u"] = _gelu_tanh(_dot(c["xn"], wmain_ref[:, M_U:M_SV]))

    def gate_logits(k):
        c = ctx[k]
        z_t = _dot_nt(wgt_ref[...], c["a_lr"]) + jnp.concatenate([bg_ref[...]] * n_blocks, axis=1)
        la_t = (jnp.minimum(z_t, 0.0) - jnp.log(1.0 + jnp.exp(-jnp.abs(z_t)))) * np.float32(1.0 / GATE_TAU)
        c["la_hi"] = la_t.astype(BF16)
        c["la_lo"] = (la_t - c["la_hi"].astype(F32)).astype(BF16)

    def decays(k):
        c = ctx[k]
        c["ex"], c["kd"] = [], []
        for j in range(n_blocks):
            cl = slice(LANES * j, LANES * (j + 1))
            ex = jnp.exp(_dot(jnp.concatenate([c["la_hi"][:, cl], c["la_lo"][:, cl]], axis=1), mm_ref[...]))
            c["ex"].append(ex)
            c["kd"].append((c["k_t"][:, cl] * ex[:, :LANES]).astype(BF16))

    def increments(k):
        c = ctx[k]
        c["incs"] = []
        zeros = jnp.zeros((CHUNK, GLA_DV), BF16)
        for j in range(n_blocks):
            incs = []
            for hh in range(GLA_HEADS):
                rs = slice(GLA_DK * hh, GLA_DK * (hh + 1))
                cs = slice(GLA_DV * hh, GLA_DV * (hh + 1))
                v0 = c["v"][LANES * j:LANES * j + CHUNK, cs]
                v1 = c["v"][LANES * j + CHUNK:LANES * (j + 1), cs]
                v_bd = jnp.concatenate([jnp.concatenate([v0, zeros], axis=1),
                                        jnp.concatenate([zeros, v1], axis=1)], axis=0)
                incs.append(_dot(c["kd"][j][rs], v_bd))
            c["incs"].append(incs)

    def spatial_mix(k):
        c = ctx[k]
        z_cols = []
        for gi in range(SG_GROUPS):
            cs = slice(SG_GROUP_CH * gi, SG_GROUP_CH * (gi + 1))
            w_g = jnp.where(causal, ws_ref[gi], jnp.zeros((), BF16))
            rhs = jnp.concatenate([c["svn"][SG_BLOCK * j:SG_BLOCK * (j + 1), cs] for j in range(n_blocks)], axis=1)
            z = _dot(w_g, rhs)
            z_cols.append([z[:, SG_GROUP_CH * j:SG_GROUP_CH * (j + 1)] + bsb_ref[:, cs] for j in range(n_blocks)])
        z_full = jnp.concatenate(
            [jnp.concatenate([z_cols[gi][j] for gi in range(SG_GROUPS)], axis=1) for j in range(n_blocks)], axis=0)
        c["sg_out"] = (c["u"] * z_full).astype(BF16)

    def recurrence_readout(k):
        c = ctx[k]
        o_chunks = []
        for j in range(n_blocks):
            for c2 in range(2):
                ci = k * (SUB // CHUNK) + 2 * j + c2
                tot = c["ex"][j][:, LANES * (1 + c2):LANES * (2 + c2)]
                for hh in range(GLA_HEADS):
                    rs = slice(GLA_DK * hh, GLA_DK * (hh + 1))
                    cs = slice(GLA_DV * hh, GLA_DV * (hh + 1))
                    state[hh] = tot[rs] * state[hh] + c["incs"][j][hh][:, GLA_DV * c2:GLA_DV * (c2 + 1)]
                    sbd_ref[ci, rs, cs] = state[hh].astype(BF16)
                rows = slice(LANES * j + CHUNK * c2, LANES * j + CHUNK * (c2 + 1))
                o_chunks.append(_dot(c["q"][rows], sbd_ref[ci]))
        c["o"] = jnp.concatenate(o_chunks, axis=0)

    def out_proj(k):
        c = ctx[k]
        heads = []
        for hh in range(GLA_HEADS):
            oh = c["o"][:, GLA_DV * hh:GLA_DV * (hh + 1)]
            heads.append(oh * lax.rsqrt(jnp.mean(oh * oh, axis=-1, keepdims=True) + EPS))
        gla_out = (jnp.concatenate(heads, axis=-1) * onorm_ref[...] * c["gate"]).astype(BF16)
        mix = _dot(jnp.concatenate([gla_out, c["sg_out"]], axis=1), wout_ref[...])
        o_ref[0, SUB * k:SUB * (k + 1), :] = c["x"] + _rms(mix, gpost_ref[...])
        c.clear()

    pre_norm(0)
    proj_qak(0)
    proj_v(0)
    proj_sv(0)
    proj_g(0)
    proj_u(0)
    for k in range(n_sub):
        nxt = k + 1 < n_sub
        gate_logits(k)
        if nxt:
            pre_norm(k + 1)
            proj_qak(k + 1)
        decays(k)
        if nxt:
            proj_v(k + 1)
        increments(k)
        if nxt:
            proj_sv(k + 1)
        spatial_mix(k)
        recurrence_readout(k)
        if nxt:
            proj_g(k + 1)
            proj_u(k + 1)
        out_proj(k)

    for hh in range(GLA_HEADS):
        s_ref[GLA_DK * hh:GLA_DK * (hh + 1), :] = state[hh]


def _kv_kernel(mem_ref, gmem_ref, wkt_ref, wv_ref, kt_ref, v_ref):
    memn = _rms(mem_ref[0], gmem_ref[...]).astype(BF16)
    kt_ref[0] = _dot_nt(wkt_ref[...], memn).astype(BF16)
    v_ref[0] = _dot(memn, wv_ref[...]).astype(BF16)


def _attn_kernel(x_ref, gpre_ref, wq_ref, kt_ref, v_ref, wo_ref, gpost_ref, o_ref):
    n_sub = TM // SUB
    ctx = [dict() for _ in range(n_sub)]

    def q_proj(k):
        c = ctx[k]
        c["x"] = x_ref[0, SUB * k:SUB * (k + 1), :]
        xn = _rms(c["x"], gpre_ref[...]).astype(BF16)
        c["q"] = _dot(xn, wq_ref[...]).astype(BF16)

    def scores(k):
        c = ctx[k]
        c["p"], c["denom"] = [], []
        for hh in range(X_HEADS):
            cs = slice(X_HEAD_DIM * hh, X_HEAD_DIM * (hh + 1))
            sc = _dot(c["q"][:, cs], kt_ref[0, cs, :])
            p = jnp.exp(sc - jnp.max(sc, axis=-1, keepdims=True))
            c["denom"].append(jnp.sum(p, axis=-1, keepdims=True))
            c["p"].append(p.astype(BF16))

    def values(k):
        c = ctx[k]
        outs = []
        for hh in range(X_HEADS):
            cs = slice(X_HEAD_DIM * hh, X_HEAD_DIM * (hh + 1))
            outs.append((_dot(c["p"][hh], v_ref[0, :, cs]) / c["denom"][hh]).astype(BF16))
        c["o"] = jnp.concatenate(outs, axis=-1)

    def out_proj(k):
        c = ctx[k]
        xa = _dot(c["o"], wo_ref[...])
        o_ref[0, SUB * k:SUB * (k + 1), :] = c["x"] + _rms(xa, gpost_ref[...])
        c.clear()

    q_proj(0)
    scores(0)
    for k in range(n_sub):
        nxt = k + 1 < n_sub
        if nxt:
            q_proj(k + 1)
        values(k)
        if nxt:
            scores(k + 1)
        out_proj(k)


def _ffn_kernel(x_ref, gpre_ref, wgu_ref, wdown_ref, gpost_ref, o_ref):
    n_sub = TM // SUB
    ctx = [dict() for _ in range(n_sub)]

    def up_proj(k):
        c = ctx[k]
        c["x"] = x_ref[0, SUB * k:SUB * (k + 1), :]
        xn = _rms(c["x"], gpre_ref[...]).astype(BF16)
        hg = _dot(xn, wgu_ref[:, :D_FF])
        hu = _dot(xn, wgu_ref[:, D_FF:])
        c["act"] = (_silu(hg) * hu).astype(BF16)

    def down_proj(k):
        c = ctx[k]
        f = _dot(c["act"], wdown_ref[...])
        o_ref[0, SUB * k:SUB * (k + 1), :] = c["x"] + _rms(f, gpost_ref[...])
        c.clear()

    up_proj(0)
    for k in range(n_sub):
        if k + 1 < n_sub:
            up_proj(k + 1)
        down_proj(k)


def _layer_spec(shape, layer):
    zeros = (0,) * len(shape)
    return pl.BlockSpec((None,) + tuple(shape), lambda b, t: (layer,) + zeros,
                        pipeline_mode=pl.Buffered(1))


def _const_spec(shape):
    zeros = (0,) * len(shape)
    return pl.BlockSpec(tuple(shape), lambda b, t: zeros, pipeline_mode=pl.Buffered(1))


def _tile_spec():
    return pl.BlockSpec((1, TM, D_MODEL), lambda b, t: (b, t, 0))


def _params(semantics):
    return pltpu.CompilerParams(dimension_semantics=semantics, vmem_limit_bytes=VMEM_LIMIT)


def _suffix_sum_matrix():
    r = np.arange(LANES)[:, None]
    s = np.arange(LANES)[None, :]
    suffix = (r > s) & (r // CHUNK == s // CHUNK)
    tot0 = np.broadcast_to(r // CHUNK == 0, (LANES, LANES))
    tot1 = np.broadcast_to(r // CHUNK == 1, (LANES, LANES))
    m = np.concatenate([suffix, tot0, tot1], axis=1).astype(np.float32)
    return np.concatenate([m, m], axis=0)


def kernel(x, mem, norm_mix_pre, w_in, w_gate_up, b_gate, gla_onorm, sg_ln_g, sg_ln_b, w_spatial,
           b_spatial, w_out, norm_mix_post, norm_x_pre, norm_mem, w_xq, w_xkv, w_xo, norm_x_post,
           norm_ffn_pre, w_ffn_gu, w_ffn_down, norm_ffn_post):
    batch, seq, _ = x.shape
    depth = w_in.shape[0]
    assert seq % TM == 0 and TM % SUB == 0 and SUB % SG_BLOCK == 0 and SG_BLOCK == 2 * CHUNK
    grid = (batch, seq // TM)

    def cols(a, b):
        return w_in[:, :, a:b]

    w_main = jnp.concatenate(
        [cols(OFF_Q, OFF_K) * np.float32(GLA_DK ** -0.5), cols(OFF_A, OFF_U),
         jnp.zeros((depth, D_MODEL, GATE_PAD - GATE_RANK), w_in.dtype), cols(OFF_K, OFF_V),
         cols(OFF_V, OFF_G), cols(OFF_G, OFF_A), cols(OFF_U, OFF_SV), cols(OFF_SV, D_IN)],
        axis=-1).astype(BF16)
    wg_t = jnp.pad(jnp.swapaxes(w_gate_up, 1, 2),
                   ((0, 0), (0, 0), (0, GATE_PAD - GATE_RANK))).astype(BF16)
    bg = jnp.broadcast_to(b_gate[:, :, None], (depth, GLA_KW, LANES)).astype(F32)
    mm = jnp.asarray(_suffix_sum_matrix(), BF16)
    ws = w_spatial.astype(BF16)
    bsb = jnp.repeat(jnp.swapaxes(b_spatial, 1, 2), SG_GROUP_CH, axis=2).astype(F32)
    wout = w_out.astype(BF16)
    wq = (w_xq * np.float32(X_HEAD_DIM ** -0.5)).astype(BF16)
    wxk_t = jnp.swapaxes(w_xkv[:, :, :D_MODEL], 1, 2).astype(BF16)
    wxv = w_xkv[:, :, D_MODEL:].astype(BF16)
    wo = w_xo.astype(BF16)
    wgu = w_ffn_gu.astype(BF16)
    wdown = w_ffn_down.astype(BF16)

    def vec(a):
        return a.reshape(depth, 1, a.shape[-1]).astype(F32)

    g_mix_pre, g_mix_post = vec(norm_mix_pre), vec(norm_mix_post)
    g_x_pre, g_x_post, g_mem = vec(norm_x_pre), vec(norm_x_post), vec(norm_mem)
    g_ffn_pre, g_ffn_post = vec(norm_ffn_pre), vec(norm_ffn_post)
    onorm, lng, lnb = vec(gla_onorm), vec(sg_ln_g), vec(sg_ln_b)

    x_shape = jax.ShapeDtypeStruct(x.shape, F32)
    kt_shape = jax.ShapeDtypeStruct((batch, D_MODEL, N_MEM), BF16)
    v_shape = jax.ShapeDtypeStruct((batch, N_MEM, D_MODEL), BF16)

    for l in range(depth):
        lspec = functools.partial(_layer_spec, layer=l)
        x = pl.pallas_call(
            _mixer_kernel,
            grid=grid,
            in_specs=[_tile_spec(), lspec((1, D_MODEL)), lspec((D_MODEL, D_MAIN)),
                      lspec((GLA_KW, GATE_PAD)), lspec((GLA_KW, LANES)),
                      _const_spec((2 * LANES, 3 * LANES)), lspec((1, GLA_WIDTH)), lspec((1, SG_WIDTH)),
                      lspec((1, SG_WIDTH)), lspec((SG_GROUPS, SG_BLOCK, SG_BLOCK)),
                      lspec((SG_BLOCK, SG_WIDTH)), lspec((D_MODEL, D_MODEL)), lspec((1, D_MODEL))],
            out_specs=_tile_spec(),
            out_shape=x_shape,
            scratch_shapes=[pltpu.VMEM((GLA_KW, GLA_DV), F32),
                            pltpu.VMEM((TM // CHUNK, GLA_KW, GLA_WIDTH), BF16)],
            compiler_params=_params(("arbitrary", "arbitrary")),
            name=f"mixer_l{l}",
        )(x, g_mix_pre, w_main, wg_t, bg, mm, onorm, lng, lnb, ws, bsb, wout, g_mix_post)

        kt_mem, v_mem = pl.pallas_call(
            _kv_kernel,
            grid=(batch,),
            in_specs=[pl.BlockSpec((1, N_MEM, D_MODEL), lambda b: (b, 0, 0)),
                      pl.BlockSpec((None, 1, D_MODEL), lambda b: (l, 0, 0)),
                      pl.BlockSpec((None, D_MODEL, D_MODEL), lambda b: (l, 0, 0)),
                      pl.BlockSpec((None, D_MODEL, D_MODEL), lambda b: (l, 0, 0))],
            out_specs=[pl.BlockSpec((1, D_MODEL, N_MEM), lambda b: (b, 0, 0)),
                       pl.BlockSpec((1, N_MEM, D_MODEL), lambda b: (b, 0, 0))],
            out_shape=[kt_shape, v_shape],
            compiler_params=_params(("arbitrary",)),
            name=f"memkv_l{l}",
        )(mem, g_mem, wxk_t, wxv)

        x = pl.pallas_call(
            _attn_kernel,
            grid=grid,
            in_specs=[_tile_spec(), lspec((1, D_MODEL)), lspec((D_MODEL, D_MODEL)),
                      pl.BlockSpec((1, D_MODEL, N_MEM), lambda b, t: (b, 0, 0)),
                      pl.BlockSpec((1, N_MEM, D_MODEL), lambda b, t: (b, 0, 0)),
                      lspec((D_MODEL, D_MODEL)), lspec((1, D_MODEL))],
            out_specs=_tile_spec(),
            out_shape=x_shape,
            compiler_params=_params(("arbitrary", "arbitrary")),
            name=f"xattn_l{l}",
        )(x, g_x_pre, wq, kt_mem, v_mem, wo, g_x_post)

        x = pl.pallas_call(
            _ffn_kernel,
            grid=grid,
            in_specs=[_tile_spec(), lspec((1, D_MODEL)), lspec((D_MODEL, 2 * D_FF)),
                      lspec((D_FF, D_MODEL)), lspec((1, D_MODEL))],
            out_specs=_tile_spec(),
            out_shape=x_shape,
            compiler_params=_params(("arbitrary", "arbitrary")),
            name=f"ffn_l{l}",
        )(x, g_ffn_pre, wgu, wdown, g_ffn_post)
    return x
```

```python
import functools

import numpy as np
import jax
import jax.numpy as jnp
from jax import lax
from jax.experimental import pallas as pl
from jax.experimental.pallas import tpu as pltpu

D_MODEL = 1024
CHUNK = 64
N_MEM = 256
EPS = 1e-6
GLA_HEADS = 4
GLA_DK = 64
GLA_DV = 128
GLA_WIDTH = GLA_HEADS * GLA_DV
GLA_KW = GLA_HEADS * GLA_DK
GATE_RANK = 16
GATE_TAU = 16.0
SG_WIDTH = D_MODEL - GLA_WIDTH
SG_GROUPS = 4
SG_GROUP_CH = SG_WIDTH // SG_GROUPS
SG_BLOCK = 128
X_HEADS = 4
X_HEAD_DIM = D_MODEL // X_HEADS
D_FF = 2816
OFF_Q = 0
OFF_K = OFF_Q + GLA_KW
OFF_V = OFF_K + GLA_KW
OFF_G = OFF_V + GLA_WIDTH
OFF_A = OFF_G + GLA_WIDTH
OFF_U = OFF_A + GATE_RANK
OFF_SV = OFF_U + SG_WIDTH
D_IN = OFF_SV + SG_WIDTH

LANES = 128
GATE_PAD = LANES
M_Q = 0
M_A = M_Q + GLA_KW
M_K = M_A + GATE_PAD
M_V = M_K + GLA_KW
M_G = M_V + GLA_WIDTH
M_U = M_G + GLA_WIDTH
M_SV = M_U + SG_WIDTH
D_MAIN = M_SV + SG_WIDTH

TM = 1024
SUB = 256
VMEM_LIMIT = 56 * 1024 * 1024

F32 = jnp.float32
BF16 = jnp.bfloat16
NT_DIMS = (((1,), (1,)), ((), ()))


def _rms(x, g):
    ms = jnp.mean(x * x, axis=-1, keepdims=True)
    return x * lax.rsqrt(ms + EPS) * g


def _gelu_tanh(x):
    c = np.float32(np.sqrt(2.0 / np.pi))
    return 0.5 * x * (1.0 + jnp.tanh(c * (x + np.float32(0.044715) * (x * x * x))))


def _silu(x):
    return x / (1.0 + jnp.exp(-x))


def _dot(a, b):
    return jnp.dot(a, b, preferred_element_type=F32)


def _dot_nt(a, b):
    return lax.dot_general(a, b, NT_DIMS, preferred_element_type=F32)


def _mixer_kernel(x_ref, gpre_ref, wmain_ref, wgt_ref, bg_ref, mm_ref, onorm_ref,
                  lng_ref, lnb_ref, ws_ref, bsb_ref, wout_ref, gpost_ref, o_ref, s_ref, sbd_ref):
    n_sub = TM // SUB
    n_blocks = SUB // LANES

    @pl.when(pl.program_id(1) == 0)
    def _():
        s_ref[...] = jnp.zeros_like(s_ref)
        sbd_ref[...] = jnp.zeros_like(sbd_ref)

    state = [s_ref[GLA_DK * hh:GLA_DK * (hh + 1), :] for hh in range(GLA_HEADS)]
    ctx = [dict() for _ in range(n_sub)]
    row_chunk = lax.broadcasted_iota(jnp.int32, (SG_BLOCK, SG_BLOCK), 0) // CHUNK
    col_chunk = lax.broadcasted_iota(jnp.int32, (SG_BLOCK, SG_BLOCK), 1) // CHUNK
    causal = col_chunk <= row_chunk

    def pre_norm(k):
        c = ctx[k]
        c["xn"] = _rms(x_ref[0, SUB * k:SUB * (k + 1), :], gpre_ref[...]).astype(BF16)

    def proj_qak(k):
        c = ctx[k]
        h = _dot(c["xn"], wmain_ref[:, M_Q:M_V])
        c["q"] = h[:, M_Q:M_Q + GLA_KW].astype(BF16)
        c["a_lr"] = h[:, M_A:M_A + GATE_PAD].astype(BF16)
        c["k_t"] = h[:, M_K:M_K + GLA_KW].T

    def proj_v(k):
        c = ctx[k]
        c["v"] = _dot(c["xn"], wmain_ref[:, M_V:M_G]).astype(BF16)

    def proj_sv(k):
        c = ctx[k]
        sv = _gelu_tanh(_dot(c["xn"], wmain_ref[:, M_SV:D_MAIN]))
        mu = jnp.mean(sv, axis=-1, keepdims=True)
        svc = sv - mu
        var = jnp.mean(svc * svc, axis=-1, keepdims=True)
        c["svn"] = (svc * lax.rsqrt(var + EPS) * lng_ref[...] + lnb_ref[...]).astype(BF16)

    def proj_g(k):
        c = ctx[k]
        c["gate"] = _silu(_dot(c["xn"], wmain_ref[:, M_G:M_U]))

    def proj_u(k):
        c = ctx[k]
        c["u"] = _gelu_tanh(_dot(c["xn"], wmain_ref[:, M_U:M_SV]))

    def gate_logits(k):
        c = ctx[k]
        z_t = _dot_nt(wgt_ref[...], c["a_lr"]) + jnp.concatenate([bg_ref[...]] * n_blocks, axis=1)
        la_t = (jnp.minimum(z_t, 0.0) - jnp.log(1.0 + jnp.exp(-jnp.abs(z_t)))) * np.float32(1.0 / GATE_TAU)
        c["la_hi"] = la_t.astype(BF16)
        c["la_lo"] = (la_t - c["la_hi"].astype(F32)).astype(BF16)

    def decays(k):
        c = ctx[k]
        c["ex"], c["kd"] = [], []
        for j in range(n_blocks):
            cl = slice(LANES * j, LANES * (j + 1))
            ex = jnp.exp(_dot(jnp.concatenate([c["la_hi"][:, cl], c["la_lo"][:, cl]], axis=1), mm_ref[...]))
            c["ex"].append(ex)
            c["kd"].append((c["k_t"][:, cl] * ex[:, :LANES]).astype(BF16))

    def increments(k):
        c = ctx[k]
        c["incs"] = []
        zeros = jnp.zeros((CHUNK, GLA_DV), BF16)
        for j in range(n_blocks):
            incs = []
            for hh in range(GLA_HEADS):
                rs = slice(GLA_DK * hh, GLA_DK * (hh + 1))
                cs = slice(GLA_DV * hh, GLA_DV * (hh + 1))
                v0 = c["v"][LANES * j:LANES * j + CHUNK, cs]
                v1 = c["v"][LANES * j + CHUNK:LANES * (j + 1), cs]
                v_bd = jnp.concatenate([jnp.concatenate([v0, zeros], axis=1),
                                        jnp.concatenate([zeros, v1], axis=1)], axis=0)
                incs.append(_dot(c["kd"][j][rs], v_bd))
            c["incs"].append(incs)

    def spatial_mix(k):
        c = ctx[k]
        z_cols = []
        for gi in range(SG_GROUPS):
            cs = slice(SG_GROUP_CH * gi, SG_GROUP_CH * (gi + 1))
            w_g = jnp.where(causal, ws_ref[gi], jnp.zeros((), BF16))
            rhs = jnp.concatenate([c["svn"][SG_BLOCK * j:SG_BLOCK * (j + 1), cs] for j in range(n_blocks)], axis=1)
            z = _dot(w_g, rhs)
            z_cols.append([z[:, SG_GROUP_CH * j:SG_GROUP_CH * (j + 1)] + bsb_ref[:, cs] for j in range(n_blocks)])
        z_full = jnp.concatenate(
            [jnp.concatenate([z_cols[gi][j] for gi in range(SG_GROUPS)], axis=1) for j in range(n_blocks)], axis=0)
        c["sg_out"] = (c["u"] * z_full).astype(BF16)

    def recurrence_readout(k):
        c = ctx[k]
        o_chunks = []
        for j in range(n_blocks):
            for c2 in range(2):
                ci = k * (SUB // CHUNK) + 2 * j + c2
                tot = c["ex"][j][:, LANES * (1 + c2):LANES * (2 + c2)]
                for hh in range(GLA_HEADS):
                    rs = slice(GLA_DK * hh, GLA_DK * (hh + 1))
                    cs = slice(GLA_DV * hh, GLA_DV * (hh + 1))
                    state[hh] = tot[rs] * state[hh] + c["incs"][j][hh][:, GLA_DV * c2:GLA_DV * (c2 + 1)]
                    sbd_ref[ci, rs, cs] = state[hh].astype(BF16)
                rows = slice(LANES * j + CHUNK * c2, LANES * j + CHUNK * (c2 + 1))
                o_chunks.append(_dot(c["q"][rows], sbd_ref[ci]))
        c["o"] = jnp.concatenate(o_chunks, axis=0)

    def out_proj(k):
        c = ctx[k]
        heads = []
        for hh in range(GLA_HEADS):
            oh = c["o"][:, GLA_DV * hh:GLA_DV * (hh + 1)]
            heads.append(oh * lax.rsqrt(jnp.mean(oh * oh, axis=-1, keepdims=True) + EPS))
        gla_out = (jnp.concatenate(heads, axis=-1) * onorm_ref[...] * c["gate"]).astype(BF16)
        mix = _dot(jnp.concatenate([gla_out, c["sg_out"]], axis=1), wout_ref[...])
        rows = slice(SUB * k, SUB * (k + 1))
        o_ref[0, rows, :] = x_ref[0, rows, :] + _rms(mix, gpost_ref[...])
        c.clear()

    pre_norm(0)
    proj_qak(0)
    proj_v(0)
    proj_sv(0)
    proj_g(0)
    proj_u(0)
    for k in range(n_sub):
        nxt = k + 1 < n_sub
        gate_logits(k)
        if nxt:
            pre_norm(k + 1)
            proj_qak(k + 1)
        decays(k)
        if nxt:
            proj_v(k + 1)
        increments(k)
        if nxt:
            proj_sv(k + 1)
        spatial_mix(k)
        recurrence_readout(k)
        if nxt:
            proj_g(k + 1)
            proj_u(k + 1)
        out_proj(k)

    for hh in range(GLA_HEADS):
        s_ref[GLA_DK * hh:GLA_DK * (hh + 1), :] = state[hh]


def _kv_kernel(mem_ref, gmem_ref, wkv_ref, k_ref, v_ref):
    memn = _rms(mem_ref[0], gmem_ref[...]).astype(BF16)
    kv = _dot(memn, wkv_ref[...])
    k_ref[0] = kv[:, :D_MODEL].astype(BF16)
    v_ref[0] = kv[:, D_MODEL:].astype(BF16)


def _attn_kernel(x_ref, gpre_ref, wq_ref, k_ref, v_ref, wo_ref, gpost_ref, o_ref):
    n_sub = TM // SUB
    ctx = [dict() for _ in range(n_sub)]

    def q_proj(k):
        c = ctx[k]
        xn = _rms(x_ref[0, SUB * k:SUB * (k + 1), :], gpre_ref[...]).astype(BF16)
        c["q"] = _dot(xn, wq_ref[...]).astype(BF16)

    def scores(k):
        c = ctx[k]
        c["p"], c["denom"] = [], []
        for hh in range(X_HEADS):
            cs = slice(X_HEAD_DIM * hh, X_HEAD_DIM * (hh + 1))
            sc = _dot_nt(c["q"][:, cs], k_ref[0, :, cs])
            p = jnp.exp(sc - jnp.max(sc, axis=-1, keepdims=True))
            c["denom"].append(jnp.sum(p, axis=-1, keepdims=True))
            c["p"].append(p.astype(BF16))

    def values(k):
        c = ctx[k]
        outs = []
        for hh in range(X_HEADS):
            cs = slice(X_HEAD_DIM * hh, X_HEAD_DIM * (hh + 1))
            outs.append((_dot(c["p"][hh], v_ref[0, :, cs]) / c["denom"][hh]).astype(BF16))
        c["o"] = jnp.concatenate(outs, axis=-1)

    def out_proj(k):
        c = ctx[k]
        xa = _dot(c["o"], wo_ref[...])
        rows = slice(SUB * k, SUB * (k + 1))
        o_ref[0, rows, :] = x_ref[0, rows, :] + _rms(xa, gpost_ref[...])
        c.clear()

    q_proj(0)
    scores(0)
    for k in range(n_sub):
        nxt = k + 1 < n_sub
        if nxt:
            q_proj(k + 1)
        values(k)
        if nxt:
            scores(k + 1)
        out_proj(k)


def _ffn_kernel(x_ref, gpre_ref, wgu_ref, wdown_ref, gpost_ref, o_ref):
    n_sub = TM // SUB
    ctx = [dict() for _ in range(n_sub)]

    def up_proj(k):
        c = ctx[k]
        xn = _rms(x_ref[0, SUB * k:SUB * (k + 1), :], gpre_ref[...]).astype(BF16)
        acts = []
        for t in range(D_FF // LANES):
            h = _dot(xn, wgu_ref[:, 2 * LANES * t:2 * LANES * (t + 1)])
            acts.append((_silu(h[:, :LANES]) * h[:, LANES:]).astype(BF16))
        c["act"] = jnp.concatenate(acts, axis=1)

    def down_proj(k):
        c = ctx[k]
        f = _dot(c["act"], wdown_ref[...])
        rows = slice(SUB * k, SUB * (k + 1))
        o_ref[0, rows, :] = x_ref[0, rows, :] + _rms(f, gpost_ref[...])
        c.clear()

    up_proj(0)
    for k in range(n_sub):
        if k + 1 < n_sub:
            up_proj(k + 1)
        down_proj(k)


def _layer_spec(shape, layer):
    zeros = (0,) * len(shape)
    return pl.BlockSpec((None,) + tuple(shape), lambda b, t: (layer,) + zeros,
                        pipeline_mode=pl.Buffered(1))


def _const_spec(shape):
    zeros = (0,) * len(shape)
    return pl.BlockSpec(tuple(shape), lambda b, t: zeros, pipeline_mode=pl.Buffered(1))


def _tile_spec():
    return pl.BlockSpec((1, TM, D_MODEL), lambda b, t: (b, t, 0))


def _params(semantics):
    return pltpu.CompilerParams(dimension_semantics=semantics, vmem_limit_bytes=VMEM_LIMIT)


def _suffix_sum_matrix():
    r = np.arange(LANES)[:, None]
    s = np.arange(LANES)[None, :]
    suffix = (r > s) & (r // CHUNK == s // CHUNK)
    tot0 = np.broadcast_to(r // CHUNK == 0, (LANES, LANES))
    tot1 = np.broadcast_to(r // CHUNK == 1, (LANES, LANES))
    m = np.concatenate([suffix, tot0, tot1], axis=1).astype(np.float32)
    return np.concatenate([m, m], axis=0)


def kernel(x, mem, norm_mix_pre, w_in, w_gate_up, b_gate, gla_onorm, sg_ln_g, sg_ln_b, w_spatial,
           b_spatial, w_out, norm_mix_post, norm_x_pre, norm_mem, w_xq, w_xkv, w_xo, norm_x_post,
           norm_ffn_pre, w_ffn_gu, w_ffn_down, norm_ffn_post):
    batch, seq, _ = x.shape
    depth = w_in.shape[0]
    assert seq % TM == 0 and TM % SUB == 0 and SUB % SG_BLOCK == 0 and SG_BLOCK == 2 * CHUNK
    grid = (batch, seq // TM)

    def cols(a, b):
        return w_in[:, :, a:b]

    w_main = jnp.concatenate(
        [cols(OFF_Q, OFF_K) * np.float32(GLA_DK ** -0.5), cols(OFF_A, OFF_U),
         jnp.zeros((depth, D_MODEL, GATE_PAD - GATE_RANK), w_in.dtype), cols(OFF_K, OFF_V),
         cols(OFF_V, OFF_G), cols(OFF_G, OFF_A), cols(OFF_U, OFF_SV), cols(OFF_SV, D_IN)],
        axis=-1).astype(BF16)
    wg_t = jnp.pad(jnp.swapaxes(w_gate_up, 1, 2),
                   ((0, 0), (0, 0), (0, GATE_PAD - GATE_RANK))).astype(BF16)
    bg = jnp.broadcast_to(b_gate[:, :, None], (depth, GLA_KW, LANES)).astype(F32)
    mm = jnp.asarray(_suffix_sum_matrix(), BF16)
    ws = w_spatial.astype(BF16)
    bsb = jnp.repeat(jnp.swapaxes(b_spatial, 1, 2), SG_GROUP_CH, axis=2).astype(F32)
    wout = w_out.astype(BF16)
    wq = (w_xq * np.float32(X_HEAD_DIM ** -0.5)).astype(BF16)
    wkv = w_xkv.astype(BF16)
    wo = w_xo.astype(BF16)
    wgu = jnp.swapaxes(w_ffn_gu.reshape(depth, D_MODEL, 2, D_FF // LANES, LANES), 2, 3)
    wgu = wgu.reshape(depth, D_MODEL, 2 * D_FF).astype(BF16)
    wdown = w_ffn_down.astype(BF16)

    def vec(a):
        return a.reshape(depth, 1, a.shape[-1]).astype(F32)

    g_mix_pre, g_mix_post = vec(norm_mix_pre), vec(norm_mix_post)
    g_x_pre, g_x_post, g_mem = vec(norm_x_pre), vec(norm_x_post), vec(norm_mem)
    g_ffn_pre, g_ffn_post = vec(norm_ffn_pre), vec(norm_ffn_post)
    onorm, lng, lnb = vec(gla_onorm), vec(sg_ln_g), vec(sg_ln_b)

    x_shape = jax.ShapeDtypeStruct(x.shape, F32)
    kv_shape = jax.ShapeDtypeStruct((batch, N_MEM, D_MODEL), BF16)

    for l in range(depth):
        lspec = functools.partial(_layer_spec, layer=l)
        x = pl.pallas_call(
            _mixer_kernel,
            grid=grid,
            in_specs=[_tile_spec(), lspec((1, D_MODEL)), lspec((D_MODEL, D_MAIN)),
                      lspec((GLA_KW, GATE_PAD)), lspec((GLA_KW, LANES)),
                      _const_spec((2 * LANES, 3 * LANES)), lspec((1, GLA_WIDTH)), lspec((1, SG_WIDTH)),
                      lspec((1, SG_WIDTH)), lspec((SG_GROUPS, SG_BLOCK, SG_BLOCK)),
                      lspec((SG_BLOCK, SG_WIDTH)), lspec((D_MODEL, D_MODEL)), lspec((1, D_MODEL))],
            out_specs=_tile_spec(),
            out_shape=x_shape,
            scratch_shapes=[pltpu.VMEM((GLA_KW, GLA_DV), F32),
                            pltpu.VMEM((TM // CHUNK, GLA_KW, GLA_WIDTH), BF16)],
            compiler_params=_params(("arbitrary", "arbitrary")),
            name=f"mixer_l{l}",
        )(x, g_mix_pre, w_main, wg_t, bg, mm, onorm, lng, lnb, ws, bsb, wout, g_mix_post)

        k_mem, v_mem = pl.pallas_call(
            _kv_kernel,
            grid=(batch,),
            in_specs=[pl.BlockSpec((1, N_MEM, D_MODEL), lambda b: (b, 0, 0)),
                      pl.BlockSpec((None, 1, D_MODEL), lambda b: (l, 0, 0)),
                      pl.BlockSpec((None, D_MODEL, 2 * D_MODEL), lambda b: (l, 0, 0))],
            out_specs=[pl.BlockSpec((1, N_MEM, D_MODEL), lambda b: (b, 0, 0)),
                       pl.BlockSpec((1, N_MEM, D_MODEL), lambda b: (b, 0, 0))],
            out_shape=[kv_shape, kv_shape],
            compiler_params=_params(("arbitrary",)),
            name=f"memkv_l{l}",
        )(mem, g_mem, wkv)

        x = pl.pallas_call(
            _attn_kernel,
            grid=grid,
            in_specs=[_tile_spec(), lspec((1, D_MODEL)), lspec((D_MODEL, D_MODEL)),
                      pl.BlockSpec((1, N_MEM, D_MODEL), lambda b, t: (b, 0, 0)),
                      pl.BlockSpec((1, N_MEM, D_MODEL), lambda b, t: (b, 0, 0)),
                      lspec((D_MODEL, D_MODEL)), lspec((1, D_MODEL))],
            out_specs=_tile_spec(),
            out_shape=x_shape,
            compiler_params=_params(("arbitrary", "arbitrary")),
            name=f"xattn_l{l}",
        )(x, g_x_pre, wq, k_mem, v_mem, wo, g_x_post)

        x = pl.pallas_call(
            _ffn_kernel,
            grid=grid,
            in_specs=[_tile_spec(), lspec((1, D_MODEL)), lspec((D_MODEL, 2 * D_FF)),
                      lspec((D_FF, D_MODEL)), lspec((1, D_MODEL))],
            out_specs=_tile_spec(),
            out_shape=x_shape,
            compiler_params=_params(("arbitrary", "arbitrary")),
            name=f"ffn_l{l}",
        )(x, g_ffn_pre, wgu, wdown, g_ffn_post)
    return x
```

```python
import functools

import numpy as np
import jax
import jax.numpy as jnp
from jax import lax
from jax.experimental import pallas as pl
from jax.experimental.pallas import tpu as pltpu

D_MODEL = 1024
CHUNK = 64
N_MEM = 256
EPS = 1e-6
GLA_HEADS = 4
GLA_DK = 64
GLA_DV = 128
GLA_WIDTH = GLA_HEADS * GLA_DV
GLA_KW = GLA_HEADS * GLA_DK
GATE_RANK = 16
GATE_TAU = 16.0
SG_WIDTH = D_MODEL - GLA_WIDTH
SG_GROUPS = 4
SG_GROUP_CH = SG_WIDTH // SG_GROUPS
SG_BLOCK = 128
X_HEADS = 4
X_HEAD_DIM = D_MODEL // X_HEADS
D_FF = 2816
OFF_Q = 0
OFF_K = OFF_Q + GLA_KW
OFF_V = OFF_K + GLA_KW
OFF_G = OFF_V + GLA_WIDTH
OFF_A = OFF_G + GLA_WIDTH
OFF_U = OFF_A + GATE_RANK
OFF_SV = OFF_U + SG_WIDTH
D_IN = OFF_SV + SG_WIDTH

LANES = 128
GATE_PAD = LANES
M_Q = 0
M_A = M_Q + GLA_KW
M_K = M_A + GATE_PAD
M_V = M_K + GLA_KW
M_G = M_V + GLA_WIDTH
M_U = M_G + GLA_WIDTH
M_SV = M_U + SG_WIDTH
D_MAIN = M_SV + SG_WIDTH

TM_MIXER = 2048
TM_ATTN = 2048
TM_FFN = 1024
SBD_SLOTS = 16
SUB = 256
OUT_PIECE = 256
VMEM_LIMIT = 56 * 1024 * 1024

F32 = jnp.float32
BF16 = jnp.bfloat16
NT_DIMS = (((1,), (1,)), ((), ()))


def _rms(x, g):
    ms = jnp.mean(x * x, axis=-1, keepdims=True)
    return x * lax.rsqrt(ms + EPS) * g


def _gelu_tanh(x):
    c = np.float32(np.sqrt(2.0 / np.pi))
    return 0.5 * x * (1.0 + jnp.tanh(c * (x + np.float32(0.044715) * (x * x * x))))


def _silu(x):
    return x / (1.0 + jnp.exp(-x))


def _dot(a, b):
    return jnp.dot(a, b, preferred_element_type=F32)


def _dot_nt(a, b):
    return lax.dot_general(a, b, NT_DIMS, preferred_element_type=F32)


def _mixer_kernel(x_ref, gpre_ref, wmain_ref, wgt_ref, bg_ref, mm_ref, onorm_ref,
                  lng_ref, lnb_ref, ws_ref, bsb_ref, wout_ref, gpost_ref, o_ref, s_ref, sbd_ref):
    n_sub = TM_MIXER // SUB
    n_blocks = SUB // LANES

    @pl.when(pl.program_id(1) == 0)
    def _():
        s_ref[...] = jnp.zeros_like(s_ref)
        sbd_ref[...] = jnp.zeros_like(sbd_ref)

    state = [s_ref[GLA_DK * hh:GLA_DK * (hh + 1), :] for hh in range(GLA_HEADS)]
    ctx = [dict() for _ in range(n_sub)]
    row_chunk = lax.broadcasted_iota(jnp.int32, (SG_BLOCK, SG_BLOCK), 0) // CHUNK
    col_chunk = lax.broadcasted_iota(jnp.int32, (SG_BLOCK, SG_BLOCK), 1) // CHUNK
    causal = col_chunk <= row_chunk

    def pre_norm(k):
        c = ctx[k]
        c["xn"] = _rms(x_ref[0, SUB * k:SUB * (k + 1), :], gpre_ref[...]).astype(BF16)

    def proj_qak(k):
        c = ctx[k]
        h = _dot(c["xn"], wmain_ref[:, M_Q:M_V])
        c["q"] = h[:, M_Q:M_Q + GLA_KW].astype(BF16)
        c["a_lr"] = h[:, M_A:M_A + GATE_PAD].astype(BF16)
        c["k_t"] = h[:, M_K:M_K + GLA_KW].T

    def proj_v(k):
        c = ctx[k]
        c["v"] = _dot(c["xn"], wmain_ref[:, M_V:M_G]).astype(BF16)

    def proj_sv(k):
        c = ctx[k]
        sv = _gelu_tanh(_dot(c["xn"], wmain_ref[:, M_SV:D_MAIN]))
        mu = jnp.mean(sv, axis=-1, keepdims=True)
        svc = sv - mu
        var = jnp.mean(svc * svc, axis=-1, keepdims=True)
        c["svn"] = (svc * lax.rsqrt(var + EPS) * lng_ref[...] + lnb_ref[...]).astype(BF16)

    def proj_g(k):
        c = ctx[k]
        c["gate"] = _silu(_dot(c["xn"], wmain_ref[:, M_G:M_U]))

    def proj_u(k):
        c = ctx[k]
        c["u"] = _gelu_tanh(_dot(c["xn"], wmain_ref[:, M_U:M_SV]))

    def gate_logits(k):
        c = ctx[k]
        z_t = _dot_nt(wgt_ref[...], c["a_lr"]) + jnp.concatenate([bg_ref[...]] * n_blocks, axis=1)
        la_t = (jnp.minimum(z_t, 0.0) - jnp.log(1.0 + jnp.exp(-jnp.abs(z_t)))) * np.float32(1.0 / GATE_TAU)
        c["la_hi"] = la_t.astype(BF16)
        c["la_lo"] = (la_t - c["la_hi"].astype(F32)).astype(BF16)

    def decays(k):
        c = ctx[k]
        c["ex"], c["kd"] = [], []
        for j in range(n_blocks):
            cl = slice(LANES * j, LANES * (j + 1))
            ex = jnp.exp(_dot(jnp.concatenate([c["la_hi"][:, cl], c["la_lo"][:, cl]], axis=1), mm_ref[...]))
            c["ex"].append(ex)
            c["kd"].append((c["k_t"][:, cl] * ex[:, :LANES]).astype(BF16))

    def increments(k):
        c = ctx[k]
        c["incs"] = []
        zeros = jnp.zeros((CHUNK, GLA_DV), BF16)
        for j in range(n_blocks):
            incs = []
            for hh in range(GLA_HEADS):
                rs = slice(GLA_DK * hh, GLA_DK * (hh + 1))
                cs = slice(GLA_DV * hh, GLA_DV * (hh + 1))
                v0 = c["v"][LANES * j:LANES * j + CHUNK, cs]
                v1 = c["v"][LANES * j + CHUNK:LANES * (j + 1), cs]
                v_bd = jnp.concatenate([jnp.concatenate([v0, zeros], axis=1),
                                        jnp.concatenate([zeros, v1], axis=1)], axis=0)
                incs.append(_dot(c["kd"][j][rs], v_bd))
            c["incs"].append(incs)

    def spatial_mix(k):
        c = ctx[k]
        z_cols = []
        for gi in range(SG_GROUPS):
            cs = slice(SG_GROUP_CH * gi, SG_GROUP_CH * (gi + 1))
            w_g = jnp.where(causal, ws_ref[gi], jnp.zeros((), BF16))
            rhs = jnp.concatenate([c["svn"][SG_BLOCK * j:SG_BLOCK * (j + 1), cs] for j in range(n_blocks)], axis=1)
            z = _dot(w_g, rhs)
            z_cols.append([z[:, SG_GROUP_CH * j:SG_GROUP_CH * (j + 1)] + bsb_ref[:, cs] for j in range(n_blocks)])
        z_full = jnp.concatenate(
            [jnp.concatenate([z_cols[gi][j] for gi in range(SG_GROUPS)], axis=1) for j in range(n_blocks)], axis=0)
        c["sg_out"] = (c["u"] * z_full).astype(BF16)

    def recurrence_readout(k):
        c = ctx[k]
        o_chunks = []
        for j in range(n_blocks):
            for c2 in range(2):
                ci = (k * (SUB // CHUNK) + 2 * j + c2) % SBD_SLOTS
                tot = c["ex"][j][:, LANES * (1 + c2):LANES * (2 + c2)]
                for hh in range(GLA_HEADS):
                    rs = slice(GLA_DK * hh, GLA_DK * (hh + 1))
                    cs = slice(GLA_DV * hh, GLA_DV * (hh + 1))
                    state[hh] = tot[rs] * state[hh] + c["incs"][j][hh][:, GLA_DV * c2:GLA_DV * (c2 + 1)]
                    sbd_ref[ci, rs, cs] = state[hh].astype(BF16)
                rows = slice(LANES * j + CHUNK * c2, LANES * j + CHUNK * (c2 + 1))
                o_chunks.append(_dot(c["q"][rows], sbd_ref[ci]))
        c["o"] = jnp.concatenate(o_chunks, axis=0)

    def mix_input(k):
        c = ctx[k]
        heads = []
        for hh in range(GLA_HEADS):
            oh = c["o"][:, GLA_DV * hh:GLA_DV * (hh + 1)]
            heads.append(oh * lax.rsqrt(jnp.mean(oh * oh, axis=-1, keepdims=True) + EPS))
        gla_out = (jnp.concatenate(heads, axis=-1) * onorm_ref[...] * c["gate"]).astype(BF16)
        c["lhs"] = jnp.concatenate([gla_out, c["sg_out"]], axis=1)
        c["mix"] = []

    def out_mm(k, pieces):
        c = ctx[k]
        for p in pieces:
            c["mix"].append(_dot(c["lhs"], wout_ref[:, OUT_PIECE * p:OUT_PIECE * (p + 1)]))

    def epilogue(k):
        c = ctx[k]
        mix = jnp.concatenate(c["mix"], axis=1)
        rows = slice(SUB * k, SUB * (k + 1))
        o_ref[0, rows, :] = x_ref[0, rows, :] + _rms(mix, gpost_ref[...])
        c.clear()

    pre_norm(0)
    proj_qak(0)
    proj_v(0)
    proj_sv(0)
    proj_g(0)
    proj_u(0)
    all_pieces = tuple(range(D_MODEL // OUT_PIECE))
    for k in range(n_sub):
        nxt = k + 1 < n_sub
        last = n_sub > 1 and not nxt
        gate_logits(k)
        if nxt:
            pre_norm(k + 1)
            proj_qak(k + 1)
        if last:
            out_mm(k - 1, all_pieces[:2])
        decays(k)
        if nxt:
            proj_v(k + 1)
        if last:
            out_mm(k - 1, all_pieces[2:3])
        increments(k)
        if nxt:
            proj_sv(k + 1)
        if last:
            out_mm(k - 1, all_pieces[3:])
            epilogue(k - 1)
        spatial_mix(k)
        recurrence_readout(k)
        if nxt:
            proj_g(k + 1)
            proj_u(k + 1)
        mix_input(k)
        if k + 2 == n_sub:
            continue
        out_mm(k, all_pieces)
        epilogue(k)

    for hh in range(GLA_HEADS):
        s_ref[GLA_DK * hh:GLA_DK * (hh + 1), :] = state[hh]


def _kv_kernel(mem_ref, gmem_ref, wkv_ref, k_ref, v_ref):
    memn = _rms(mem_ref[0], gmem_ref[...]).astype(BF16)
    kv = _dot(memn, wkv_ref[...])
    k_ref[0] = kv[:, :D_MODEL].astype(BF16)
    v_ref[0] = kv[:, D_MODEL:].astype(BF16)


def _attn_kernel(x_ref, gpre_ref, wq_ref, k_ref, v_ref, wo_ref, gpost_ref, o_ref):
    n_sub = TM_ATTN // SUB
    ctx = [dict() for _ in range(n_sub)]

    def q_proj(k):
        c = ctx[k]
        xn = _rms(x_ref[0, SUB * k:SUB * (k + 1), :], gpre_ref[...]).astype(BF16)
        c["q"] = _dot(xn, wq_ref[...]).astype(BF16)

    def scores(k):
        c = ctx[k]
        c["p"], c["denom"] = [], []
        for hh in range(X_HEADS):
            cs = slice(X_HEAD_DIM * hh, X_HEAD_DIM * (hh + 1))
            sc = _dot_nt(c["q"][:, cs], k_ref[0, :, cs])
            p = jnp.exp(sc - jnp.max(sc, axis=-1, keepdims=True))
            c["denom"].append(jnp.sum(p, axis=-1, keepdims=True))
            c["p"].append(p.astype(BF16))

    def values(k):
        c = ctx[k]
        outs = []
        for hh in range(X_HEADS):
            cs = slice(X_HEAD_DIM * hh, X_HEAD_DIM * (hh + 1))
            outs.append((_dot(c["p"][hh], v_ref[0, :, cs]) / c["denom"][hh]).astype(BF16))
        c["o"] = jnp.concatenate(outs, axis=-1)

    def out_mm(k, pieces):
        c = ctx[k]
        c.setdefault("xa", [])
        for p in pieces:
            c["xa"].append(_dot(c["o"], wo_ref[:, OUT_PIECE * p:OUT_PIECE * (p + 1)]))

    def epilogue(k):
        c = ctx[k]
        xa = jnp.concatenate(c["xa"], axis=1)
        rows = slice(SUB * k, SUB * (k + 1))
        o_ref[0, rows, :] = x_ref[0, rows, :] + _rms(xa, gpost_ref[...])
        c.clear()

    all_pieces = tuple(range(D_MODEL // OUT_PIECE))
    q_proj(0)
    scores(0)
    for k in range(n_sub):
        nxt = k + 1 < n_sub
        last = n_sub > 1 and not nxt
        if nxt:
            q_proj(k + 1)
        if last:
            out_mm(k - 1, all_pieces[:2])
        values(k)
        if nxt:
            scores(k + 1)
        if last:
            out_mm(k - 1, all_pieces[2:])
            epilogue(k - 1)
        if k + 2 == n_sub:
            continue
        out_mm(k, all_pieces)
        epilogue(k)


def _ffn_kernel(x_ref, gpre_ref, wgu_ref, wdown_ref, gpost_ref, o_ref):
    n_sub = TM_FFN // SUB
    ctx = [dict() for _ in range(n_sub)]

    def up_proj(k):
        c = ctx[k]
        xn = _rms(x_ref[0, SUB * k:SUB * (k + 1), :], gpre_ref[...]).astype(BF16)
        hg = _dot(xn, wgu_ref[:, :D_FF])
        hu = _dot(xn, wgu_ref[:, D_FF:])
        c["act"] = (_silu(hg) * hu).astype(BF16)

    def down_proj(k):
        c = ctx[k]
        f = _dot(c["act"], wdown_ref[...])
        rows = slice(SUB * k, SUB * (k + 1))
        o_ref[0, rows, :] = x_ref[0, rows, :] + _rms(f, gpost_ref[...])
        c.clear()

    up_proj(0)
    for k in range(n_sub):
        if k + 1 < n_sub:
            up_proj(k + 1)
        down_proj(k)


def _layer_spec(shape, layer):
    zeros = (0,) * len(shape)
    return pl.BlockSpec((None,) + tuple(shape), lambda b, t: (layer,) + zeros,
                        pipeline_mode=pl.Buffered(1))


def _const_spec(shape):
    zeros = (0,) * len(shape)
    return pl.BlockSpec(tuple(shape), lambda b, t: zeros, pipeline_mode=pl.Buffered(1))


def _tile_spec(tm):
    return pl.BlockSpec((1, tm, D_MODEL), lambda b, t: (b, t, 0))


def _params(semantics):
    return pltpu.CompilerParams(dimension_semantics=semantics, vmem_limit_bytes=VMEM_LIMIT)


def _suffix_sum_matrix():
    r = np.arange(LANES)[:, None]
    s = np.arange(LANES)[None, :]
    suffix = (r > s) & (r // CHUNK == s // CHUNK)
    tot0 = np.broadcast_to(r // CHUNK == 0, (LANES, LANES))
    tot1 = np.broadcast_to(r // CHUNK == 1, (LANES, LANES))
    m = np.concatenate([suffix, tot0, tot1], axis=1).astype(np.float32)
    return np.concatenate([m, m], axis=0)


def kernel(x, mem, norm_mix_pre, w_in, w_gate_up, b_gate, gla_onorm, sg_ln_g, sg_ln_b, w_spatial,
           b_spatial, w_out, norm_mix_post, norm_x_pre, norm_mem, w_xq, w_xkv, w_xo, norm_x_post,
           norm_ffn_pre, w_ffn_gu, w_ffn_down, norm_ffn_post):
    batch, seq, _ = x.shape
    depth = w_in.shape[0]
    assert SUB % SG_BLOCK == 0 and SG_BLOCK == 2 * CHUNK
    for tm in (TM_MIXER, TM_ATTN, TM_FFN):
        assert seq % tm == 0 and tm % SUB == 0

    def cols(a, b):
        return w_in[:, :, a:b]

    w_main = jnp.concatenate(
        [cols(OFF_Q, OFF_K) * np.float32(GLA_DK ** -0.5), cols(OFF_A, OFF_U),
         jnp.zeros((depth, D_MODEL, GATE_PAD - GATE_RANK), w_in.dtype), cols(OFF_K, OFF_V),
         cols(OFF_V, OFF_G), cols(OFF_G, OFF_A), cols(OFF_U, OFF_SV), cols(OFF_SV, D_IN)],
        axis=-1).astype(BF16)
    wg_t = jnp.pad(jnp.swapaxes(w_gate_up, 1, 2),
                   ((0, 0), (0, 0), (0, GATE_PAD - GATE_RANK))).astype(BF16)
    bg = jnp.broadcast_to(b_gate[:, :, None], (depth, GLA_KW, LANES)).astype(F32)
    mm = jnp.asarray(_suffix_sum_matrix(), BF16)
    ws = w_spatial.astype(BF16)
    bsb = jnp.repeat(jnp.swapaxes(b_spatial, 1, 2), SG_GROUP_CH, axis=2).astype(F32)
    wout = w_out.astype(BF16)
    wq = (w_xq * np.float32(X_HEAD_DIM ** -0.5)).astype(BF16)
    wkv = w_xkv.astype(BF16)
    wo = w_xo.astype(BF16)
    wgu = w_ffn_gu.astype(BF16)
    wdown = w_ffn_down.astype(BF16)

    def vec(a):
        return a.reshape(depth, 1, a.shape[-1]).astype(F32)

    g_mix_pre, g_mix_post = vec(norm_mix_pre), vec(norm_mix_post)
    g_x_pre, g_x_post, g_mem = vec(norm_x_pre), vec(norm_x_post), vec(norm_mem)
    g_ffn_pre, g_ffn_post = vec(norm_ffn_pre), vec(norm_ffn_post)
    onorm, lng, lnb = vec(gla_onorm), vec(sg_ln_g), vec(sg_ln_b)

    x_shape = jax.ShapeDtypeStruct(x.shape, F32)
    kv_shape = jax.ShapeDtypeStruct((batch, N_MEM, D_MODEL), BF16)

    for l in range(depth):
        lspec = functools.partial(_layer_spec, layer=l)
        x = pl.pallas_call(
            _mixer_kernel,
            grid=(batch, seq // TM_MIXER),
            in_specs=[_tile_spec(TM_MIXER), lspec((1, D_MODEL)), lspec((D_MODEL, D_MAIN)),
                      lspec((GLA_KW, GATE_PAD)), lspec((GLA_KW, LANES)),
                      _const_spec((2 * LANES, 3 * LANES)), lspec((1, GLA_WIDTH)), lspec((1, SG_WIDTH)),
                      lspec((1, SG_WIDTH)), lspec((SG_GROUPS, SG_BLOCK, SG_BLOCK)),
                      lspec((SG_BLOCK, SG_WIDTH)), lspec((D_MODEL, D_MODEL)), lspec((1, D_MODEL))],
            out_specs=_tile_spec(TM_MIXER),
            out_shape=x_shape,
            scratch_shapes=[pltpu.VMEM((GLA_KW, GLA_DV), F32),
                            pltpu.VMEM((SBD_SLOTS, GLA_KW, GLA_WIDTH), BF16)],
            compiler_params=_params(("arbitrary", "arbitrary")),
            name=f"mixer_l{l}",
        )(x, g_mix_pre, w_main, wg_t, bg, mm, onorm, lng, lnb, ws, bsb, wout, g_mix_post)

        k_mem, v_mem = pl.pallas_call(
            _kv_kernel,
            grid=(batch,),
            in_specs=[pl.BlockSpec((1, N_MEM, D_MODEL), lambda b: (b, 0, 0)),
                      pl.BlockSpec((None, 1, D_MODEL), lambda b: (l, 0, 0)),
                      pl.BlockSpec((None, D_MODEL, 2 * D_MODEL), lambda b: (l, 0, 0))],
            out_specs=[pl.BlockSpec((1, N_MEM, D_MODEL), lambda b: (b, 0, 0)),
                       pl.BlockSpec((1, N_MEM, D_MODEL), lambda b: (b, 0, 0))],
            out_shape=[kv_shape, kv_shape],
            compiler_params=_params(("arbitrary",)),
            name=f"memkv_l{l}",
        )(mem, g_mem, wkv)

        x = pl.pallas_call(
            _attn_kernel,
            grid=(batch, seq // TM_ATTN),
            in_specs=[_tile_spec(TM_ATTN), lspec((1, D_MODEL)), lspec((D_MODEL, D_MODEL)),
                      pl.BlockSpec((1, N_MEM, D_MODEL), lambda b, t: (b, 0, 0)),
                      pl.BlockSpec((1, N_MEM, D_MODEL), lambda b, t: (b, 0, 0)),
                      lspec((D_MODEL, D_MODEL)), lspec((1, D_MODEL))],
            out_specs=_tile_spec(TM_ATTN),
            out_shape=x_shape,
            compiler_params=_params(("arbitrary", "arbitrary")),
            name=f"xattn_l{l}",
        )(x, g_x_pre, wq, k_mem, v_mem, wo, g_x_post)

        x = pl.pallas_call(
            _ffn_kernel,
            grid=(batch, seq // TM_FFN),
            in_specs=[_tile_spec(TM_FFN), lspec((1, D_MODEL)), lspec((D_MODEL, 2 * D_FF)),
                      lspec((D_FF, D_MODEL)), lspec((1, D_MODEL))],
            out_specs=_tile_spec(TM_FFN),
            out_shape=x_shape,
            compiler_params=_params(("arbitrary", "arbitrary")),
            name=f"ffn_l{l}",
        )(x, g_ffn_pre, wgu, wdown, g_ffn_post)
    return x
```

```python
import functools

import numpy as np
import jax
import jax.numpy as jnp
from jax import lax
from jax.experimental import pallas as pl
from jax.experimental.pallas import tpu as pltpu

D_MODEL = 1024
CHUNK = 64
N_MEM = 256
EPS = 1e-6
GLA_HEADS = 4
GLA_DK = 64
GLA_DV = 128
GLA_WIDTH = GLA_HEADS * GLA_DV
GLA_KW = GLA_HEADS * GLA_DK
GATE_RANK = 16
GATE_TAU = 16.0
SG_WIDTH = D_MODEL - GLA_WIDTH
SG_GROUPS = 4
SG_GROUP_CH = SG_WIDTH // SG_GROUPS
SG_BLOCK = 128
X_HEADS = 4
X_HEAD_DIM = D_MODEL // X_HEADS
D_FF = 2816
OFF_Q = 0
OFF_K = OFF_Q + GLA_KW
OFF_V = OFF_K + GLA_KW
OFF_G = OFF_V + GLA_WIDTH
OFF_A = OFF_G + GLA_WIDTH
OFF_U = OFF_A + GATE_RANK
OFF_SV = OFF_U + SG_WIDTH
D_IN = OFF_SV + SG_WIDTH

LANES = 128
GATE_PAD = LANES
M_Q = 0
M_A = M_Q + GLA_KW
M_K = M_A + GATE_PAD
M_V = M_K + GLA_KW
M_G = M_V + GLA_WIDTH
M_U = M_G + GLA_WIDTH
M_SV = M_U + SG_WIDTH
D_MAIN = M_SV + SG_WIDTH

TM_MIXER = 1024
TM_ATTN = 2048
TM_FFN = 1024
SBD_SLOTS = 16
SUB = 256
OUT_PIECE = 256
VMEM_LIMIT = 56 * 1024 * 1024

F32 = jnp.float32
BF16 = jnp.bfloat16
NT_DIMS = (((1,), (1,)), ((), ()))


def _rms(x, g):
    ms = jnp.mean(x * x, axis=-1, keepdims=True)
    return x * lax.rsqrt(ms + EPS) * g


def _gelu_tanh(x):
    c = np.sqrt(2.0 / np.pi)
    hx = 0.5 * x
    return hx + hx * jnp.tanh(x * (np.float32(c) + np.float32(c * 0.044715) * (x * x)))


def _silu(x):
    return x / (1.0 + jnp.exp2(x * np.float32(-np.log2(np.e))))


def _dot(a, b):
    return jnp.dot(a, b, preferred_element_type=F32)


def _dot_nt(a, b):
    return lax.dot_general(a, b, NT_DIMS, preferred_element_type=F32)


def _mixer_kernel(x_ref, gpre_ref, wmain_ref, wgt_ref, bg_ref, mm_ref, onorm_ref,
                  lng_ref, lnb_ref, ws_ref, bsb_ref, wout_ref, gpost_ref, o_ref, s_ref, sbd_ref):
    n_sub = TM_MIXER // SUB
    n_blocks = SUB // LANES

    @pl.when(pl.program_id(1) == 0)
    def _():
        s_ref[...] = jnp.zeros_like(s_ref)
        sbd_ref[...] = jnp.zeros_like(sbd_ref)

    state = [s_ref[GLA_DK * hh:GLA_DK * (hh + 1), :] for hh in range(GLA_HEADS)]
    ctx = [dict() for _ in range(n_sub)]
    row_chunk = lax.broadcasted_iota(jnp.int32, (SG_BLOCK, SG_BLOCK), 0) // CHUNK
    col_chunk = lax.broadcasted_iota(jnp.int32, (SG_BLOCK, SG_BLOCK), 1) // CHUNK
    causal = col_chunk <= row_chunk

    def pre_norm(k):
        c = ctx[k]
        c["xn"] = _rms(x_ref[0, SUB * k:SUB * (k + 1), :], gpre_ref[...]).astype(BF16)

    def proj_qak(k):
        c = ctx[k]
        h = _dot(c["xn"], wmain_ref[:, M_Q:M_V])
        c["q"] = h[:, M_Q:M_Q + GLA_KW].astype(BF16)
        c["a_lr"] = h[:, M_A:M_A + GATE_PAD].astype(BF16)
        c["k_t"] = h[:, M_K:M_K + GLA_KW].T

    def proj_v(k):
        c = ctx[k]
        c["v"] = _dot(c["xn"], wmain_ref[:, M_V:M_G]).astype(BF16)

    def proj_sv(k):
        c = ctx[k]
        sv = _gelu_tanh(_dot(c["xn"], wmain_ref[:, M_SV:D_MAIN]))
        mu = jnp.mean(sv, axis=-1, keepdims=True)
        svc = sv - mu
        var = jnp.mean(svc * svc, axis=-1, keepdims=True)
        c["svn"] = (svc * lax.rsqrt(var + EPS) * lng_ref[...] + lnb_ref[...]).astype(BF16)

    def proj_g(k):
        c = ctx[k]
        c["gate"] = _silu(_dot(c["xn"], wmain_ref[:, M_G:M_U]))

    def proj_u(k):
        c = ctx[k]
        c["u"] = _gelu_tanh(_dot(c["xn"], wmain_ref[:, M_U:M_SV]))

    def gate_logits(k):
        c = ctx[k]
        z_t = _dot_nt(wgt_ref[...], c["a_lr"]) + jnp.concatenate([bg_ref[...]] * n_blocks, axis=1)
        la_t = (jnp.minimum(z_t, 0.0) - jnp.log(1.0 + jnp.exp(-jnp.abs(z_t)))) * np.float32(1.0 / GATE_TAU)
        c["la_hi"] = la_t.astype(BF16)
        c["la_lo"] = (la_t - c["la_hi"].astype(F32)).astype(BF16)

    def decays(k):
        c = ctx[k]
        c["ex"], c["kd"] = [], []
        for j in range(n_blocks):
            cl = slice(LANES * j, LANES * (j + 1))
            ex = jnp.exp(_dot(jnp.concatenate([c["la_hi"][:, cl], c["la_lo"][:, cl]], axis=1), mm_ref[...]))
            c["ex"].append(ex)
            c["kd"].append((c["k_t"][:, cl] * ex[:, :LANES]).astype(BF16))

    def increments(k):
        c = ctx[k]
        c["incs"] = []
        zeros = jnp.zeros((CHUNK, GLA_DV), BF16)
        for j in range(n_blocks):
            incs = []
            for hh in range(GLA_HEADS):
                rs = slice(GLA_DK * hh, GLA_DK * (hh + 1))
                cs = slice(GLA_DV * hh, GLA_DV * (hh + 1))
                v0 = c["v"][LANES * j:LANES * j + CHUNK, cs]
                v1 = c["v"][LANES * j + CHUNK:LANES * (j + 1), cs]
                v_bd = jnp.concatenate([jnp.concatenate([v0, zeros], axis=1),
                                        jnp.concatenate([zeros, v1], axis=1)], axis=0)
                incs.append(_dot(c["kd"][j][rs], v_bd))
            c["incs"].append(incs)

    def spatial_mix(k):
        c = ctx[k]
        z_cols = []
        for gi in range(SG_GROUPS):
            cs = slice(SG_GROUP_CH * gi, SG_GROUP_CH * (gi + 1))
            w_g = jnp.where(causal, ws_ref[gi], jnp.zeros((), BF16))
            rhs = jnp.concatenate([c["svn"][SG_BLOCK * j:SG_BLOCK * (j + 1), cs] for j in range(n_blocks)], axis=1)
            z = _dot(w_g, rhs)
            z_cols.append([z[:, SG_GROUP_CH * j:SG_GROUP_CH * (j + 1)] + bsb_ref[:, cs] for j in range(n_blocks)])
        z_full = jnp.concatenate(
            [jnp.concatenate([z_cols[gi][j] for gi in range(SG_GROUPS)], axis=1) for j in range(n_blocks)], axis=0)
        c["sg_out"] = (c["u"] * z_full).astype(BF16)

    def recurrence_readout(k):
        c = ctx[k]
        o_chunks = []
        for j in range(n_blocks):
            for c2 in range(2):
                ci = (k * (SUB // CHUNK) + 2 * j + c2) % SBD_SLOTS
                tot = c["ex"][j][:, LANES * (1 + c2):LANES * (2 + c2)]
                for hh in range(GLA_HEADS):
                    rs = slice(GLA_DK * hh, GLA_DK * (hh + 1))
                    cs = slice(GLA_DV * hh, GLA_DV * (hh + 1))
                    state[hh] = tot[rs] * state[hh] + c["incs"][j][hh][:, GLA_DV * c2:GLA_DV * (c2 + 1)]
                    sbd_ref[ci, rs, cs] = state[hh].astype(BF16)
                rows = slice(LANES * j + CHUNK * c2, LANES * j + CHUNK * (c2 + 1))
                o_chunks.append(_dot(c["q"][rows], sbd_ref[ci]))
        c["o"] = jnp.concatenate(o_chunks, axis=0)

    def mix_input(k):
        c = ctx[k]
        heads = []
        for hh in range(GLA_HEADS):
            oh = c["o"][:, GLA_DV * hh:GLA_DV * (hh + 1)]
            heads.append(oh * lax.rsqrt(jnp.mean(oh * oh, axis=-1, keepdims=True) + EPS))
        gla_out = (jnp.concatenate(heads, axis=-1) * onorm_ref[...] * c["gate"]).astype(BF16)
        c["lhs"] = jnp.concatenate([gla_out, c["sg_out"]], axis=1)
        c["mix"] = []

    def out_mm(k, pieces):
        c = ctx[k]
        for p in pieces:
            c["mix"].append(_dot(c["lhs"], wout_ref[:, OUT_PIECE * p:OUT_PIECE * (p + 1)]))

    def epilogue(k):
        c = ctx[k]
        mix = jnp.concatenate(c["mix"], axis=1)
        rows = slice(SUB * k, SUB * (k + 1))
        o_ref[0, rows, :] = x_ref[0, rows, :] + _rms(mix, gpost_ref[...])
        c.clear()

    pre_norm(0)
    proj_qak(0)
    proj_v(0)
    proj_sv(0)
    proj_g(0)
    proj_u(0)
    all_pieces = tuple(range(D_MODEL // OUT_PIECE))
    for k in range(n_sub):
        nxt = k + 1 < n_sub
        last = n_sub > 1 and not nxt
        gate_logits(k)
        if nxt:
            pre_norm(k + 1)
            proj_qak(k + 1)
        if last:
            out_mm(k - 1, all_pieces[:2])
        decays(k)
        if nxt:
            proj_v(k + 1)
        if last:
            out_mm(k - 1, all_pieces[2:3])
        increments(k)
        if nxt:
            proj_sv(k + 1)
        if last:
            out_mm(k - 1, all_pieces[3:])
            epilogue(k - 1)
        spatial_mix(k)
        recurrence_readout(k)
        if nxt:
            proj_g(k + 1)
            proj_u(k + 1)
        mix_input(k)
        if k + 2 == n_sub:
            continue
        out_mm(k, all_pieces)
        epilogue(k)

    for hh in range(GLA_HEADS):
        s_ref[GLA_DK * hh:GLA_DK * (hh + 1), :] = state[hh]


def _kv_kernel(mem_ref, gmem_ref, wkv_ref, k_ref, v_ref):
    memn = _rms(mem_ref[0], gmem_ref[...]).astype(BF16)
    kv = _dot(memn, wkv_ref[...])
    k_ref[0] = kv[:, :D_MODEL].astype(BF16)
    v_ref[0] = kv[:, D_MODEL:].astype(BF16)


def _attn_kernel(x_ref, gpre_ref, wq_ref, k_ref, v_ref, wo_ref, gpost_ref, o_ref):
    n_sub = TM_ATTN // SUB
    ctx = [dict() for _ in range(n_sub)]

    def q_proj(k):
        c = ctx[k]
        xn = _rms(x_ref[0, SUB * k:SUB * (k + 1), :], gpre_ref[...]).astype(BF16)
        c["q"] = _dot(xn, wq_ref[...]).astype(BF16)

    def scores(k):
        c = ctx[k]
        c["p"], c["denom"] = [], []
        for hh in range(X_HEADS):
            cs = slice(X_HEAD_DIM * hh, X_HEAD_DIM * (hh + 1))
            sc = _dot_nt(c["q"][:, cs], k_ref[0, :, cs])
            p = jnp.exp(sc - jnp.max(sc, axis=-1, keepdims=True))
            c["denom"].append(jnp.sum(p, axis=-1, keepdims=True))
            c["p"].append(p.astype(BF16))

    def values(k):
        c = ctx[k]
        outs = []
        for hh in range(X_HEADS):
            cs = slice(X_HEAD_DIM * hh, X_HEAD_DIM * (hh + 1))
            outs.append((_dot(c["p"][hh], v_ref[0, :, cs]) / c["denom"][hh]).astype(BF16))
        c["o"] = jnp.concatenate(outs, axis=-1)

    def out_mm(k, pieces):
        c = ctx[k]
        c.setdefault("xa", [])
        for p in pieces:
            c["xa"].append(_dot(c["o"], wo_ref[:, OUT_PIECE * p:OUT_PIECE * (p + 1)]))

    def epilogue(k):
        c = ctx[k]
        xa = jnp.concatenate(c["xa"], axis=1)
        rows = slice(SUB * k, SUB * (k + 1))
        o_ref[0, rows, :] = x_ref[0, rows, :] + _rms(xa, gpost_ref[...])
        c.clear()

    all_pieces = tuple(range(D_MODEL // OUT_PIECE))
    q_proj(0)
    scores(0)
    for k in range(n_sub):
        nxt = k + 1 < n_sub
        last = n_sub > 1 and not nxt
        if nxt:
            q_proj(k + 1)
        if last:
            out_mm(k - 1, all_pieces[:2])
        values(k)
        if nxt:
            scores(k + 1)
        if last:
            out_mm(k - 1, all_pieces[2:])
            epilogue(k - 1)
        if k + 2 == n_sub:
            continue
        out_mm(k, all_pieces)
        epilogue(k)


def _ffn_kernel(x_ref, gpre_ref, wgu_ref, wdown_ref, gpost_ref, o_ref):
    n_sub = TM_FFN // SUB
    ctx = [dict() for _ in range(n_sub)]

    def up_proj(k):
        c = ctx[k]
        xn = _rms(x_ref[0, SUB * k:SUB * (k + 1), :], gpre_ref[...]).astype(BF16)
        hg = _dot(xn, wgu_ref[:, :D_FF])
        hu = _dot(xn, wgu_ref[:, D_FF:])
        c["act"] = (_silu(hg) * hu).astype(BF16)

    def down_proj(k):
        c = ctx[k]
        f = _dot(c["act"], wdown_ref[...])
        rows = slice(SUB * k, SUB * (k + 1))
        o_ref[0, rows, :] = x_ref[0, rows, :] + _rms(f, gpost_ref[...])
        c.clear()

    up_proj(0)
    for k in range(n_sub):
        if k + 1 < n_sub:
            up_proj(k + 1)
        down_proj(k)


def _layer_spec(shape, layer):
    zeros = (0,) * len(shape)
    return pl.BlockSpec((None,) + tuple(shape), lambda b, t: (layer,) + zeros,
                        pipeline_mode=pl.Buffered(1))


def _const_spec(shape):
    zeros = (0,) * len(shape)
    return pl.BlockSpec(tuple(shape), lambda b, t: zeros, pipeline_mode=pl.Buffered(1))


def _tile_spec(tm):
    return pl.BlockSpec((1, tm, D_MODEL), lambda b, t: (b, t, 0))


def _params(semantics):
    return pltpu.CompilerParams(dimension_semantics=semantics, vmem_limit_bytes=VMEM_LIMIT)


def _suffix_sum_matrix():
    r = np.arange(LANES)[:, None]
    s = np.arange(LANES)[None, :]
    suffix = (r > s) & (r // CHUNK == s // CHUNK)
    tot0 = np.broadcast_to(r // CHUNK == 0, (LANES, LANES))
    tot1 = np.broadcast_to(r // CHUNK == 1, (LANES, LANES))
    m = np.concatenate([suffix, tot0, tot1], axis=1).astype(np.float32)
    return np.concatenate([m, m], axis=0)


def kernel(x, mem, norm_mix_pre, w_in, w_gate_up, b_gate, gla_onorm, sg_ln_g, sg_ln_b, w_spatial,
           b_spatial, w_out, norm_mix_post, norm_x_pre, norm_mem, w_xq, w_xkv, w_xo, norm_x_post,
           norm_ffn_pre, w_ffn_gu, w_ffn_down, norm_ffn_post):
    batch, seq, _ = x.shape
    depth = w_in.shape[0]
    assert SUB % SG_BLOCK == 0 and SG_BLOCK == 2 * CHUNK
    for tm in (TM_MIXER, TM_ATTN, TM_FFN):
        assert seq % tm == 0 and tm % SUB == 0

    def cols(a, b):
        return w_in[:, :, a:b]

    w_main = jnp.concatenate(
        [cols(OFF_Q, OFF_K) * np.float32(GLA_DK ** -0.5), cols(OFF_A, OFF_U),
         jnp.zeros((depth, D_MODEL, GATE_PAD - GATE_RANK), w_in.dtype), cols(OFF_K, OFF_V),
         cols(OFF_V, OFF_G), cols(OFF_G, OFF_A), cols(OFF_U, OFF_SV), cols(OFF_SV, D_IN)],
        axis=-1).astype(BF16)
    wg_t = jnp.pad(jnp.swapaxes(w_gate_up, 1, 2),
                   ((0, 0), (0, 0), (0, GATE_PAD - GATE_RANK))).astype(BF16)
    bg = jnp.broadcast_to(b_gate[:, :, None], (depth, GLA_KW, LANES)).astype(F32)
    mm = jnp.asarray(_suffix_sum_matrix(), BF16)
    ws = w_spatial.astype(BF16)
    bsb = jnp.repeat(jnp.swapaxes(b_spatial, 1, 2), SG_GROUP_CH, axis=2).astype(F32)
    wout = w_out.astype(BF16)
    wq = (w_xq * np.float32(X_HEAD_DIM ** -0.5)).astype(BF16)
    wkv = w_xkv.astype(BF16)
    wo = w_xo.astype(BF16)
    wgu = w_ffn_gu.astype(BF16)
    wdown = w_ffn_down.astype(BF16)

    def vec(a):
        return a.reshape(depth, 1, a.shape[-1]).astype(F32)

    g_mix_pre, g_mix_post = vec(norm_mix_pre), vec(norm_mix_post)
    g_x_pre, g_x_post, g_mem = vec(norm_x_pre), vec(norm_x_post), vec(norm_mem)
    g_ffn_pre, g_ffn_post = vec(norm_ffn_pre), vec(norm_ffn_post)
    onorm, lng, lnb = vec(gla_onorm), vec(sg_ln_g), vec(sg_ln_b)

    x_shape = jax.ShapeDtypeStruct(x.shape, F32)
    kv_shape = jax.ShapeDtypeStruct((depth, batch, N_MEM, D_MODEL), BF16)

    kv_block = pl.BlockSpec((None, 1, N_MEM, D_MODEL), lambda l, b: (l, b, 0, 0))
    k_mem, v_mem = pl.pallas_call(
        _kv_kernel,
        grid=(depth, batch),
        in_specs=[pl.BlockSpec((1, N_MEM, D_MODEL), lambda l, b: (b, 0, 0)),
                  pl.BlockSpec((None, 1, D_MODEL), lambda l, b: (l, 0, 0)),
                  pl.BlockSpec((None, D_MODEL, 2 * D_MODEL), lambda l, b: (l, 0, 0))],
        out_specs=[kv_block, kv_block],
        out_shape=[kv_shape, kv_shape],
        compiler_params=_params(("arbitrary", "arbitrary")),
        name="memkv",
    )(mem, g_mem, wkv)

    for l in range(depth):
        lspec = functools.partial(_layer_spec, layer=l)
        x = pl.pallas_call(
            _mixer_kernel,
            grid=(batch, seq // TM_MIXER),
            in_specs=[_tile_spec(TM_MIXER), lspec((1, D_MODEL)), lspec((D_MODEL, D_MAIN)),
                      lspec((GLA_KW, GATE_PAD)), lspec((GLA_KW, LANES)),
                      _const_spec((2 * LANES, 3 * LANES)), lspec((1, GLA_WIDTH)), lspec((1, SG_WIDTH)),
                      lspec((1, SG_WIDTH)), lspec((SG_GROUPS, SG_BLOCK, SG_BLOCK)),
                      lspec((SG_BLOCK, SG_WIDTH)), lspec((D_MODEL, D_MODEL)), lspec((1, D_MODEL))],
            out_specs=_tile_spec(TM_MIXER),
            out_shape=x_shape,
            scratch_shapes=[pltpu.VMEM((GLA_KW, GLA_DV), F32),
                            pltpu.VMEM((SBD_SLOTS, GLA_KW, GLA_WIDTH), BF16)],
            compiler_params=_params(("arbitrary", "arbitrary")),
            name=f"mixer_l{l}",
        )(x, g_mix_pre, w_main, wg_t, bg, mm, onorm, lng, lnb, ws, bsb, wout, g_mix_post)

        x = pl.pallas_call(
            _attn_kernel,
            grid=(batch, seq // TM_ATTN),
            in_specs=[_tile_spec(TM_ATTN), lspec((1, D_MODEL)), lspec((D_MODEL, D_MODEL)),
                      pl.BlockSpec((None, 1, N_MEM, D_MODEL), lambda b, t: (l, b, 0, 0)),
                      pl.BlockSpec((None, 1, N_MEM, D_MODEL), lambda b, t: (l, b, 0, 0)),
                      lspec((D_MODEL, D_MODEL)), lspec((1, D_MODEL))],
            out_specs=_tile_spec(TM_ATTN),
            out_shape=x_shape,
            compiler_params=_params(("arbitrary", "arbitrary")),
            name=f"xattn_l{l}",
        )(x, g_x_pre, wq, k_mem, v_mem, wo, g_x_post)

        x = pl.pallas_call(
            _ffn_kernel,
            grid=(batch, seq // TM_FFN),
            in_specs=[_tile_spec(TM_FFN), lspec((1, D_MODEL)), lspec((D_MODEL, 2 * D_FF)),
                      lspec((D_FF, D_MODEL)), lspec((1, D_MODEL))],
            out_specs=_tile_spec(TM_FFN),
            out_shape=x_shape,
            compiler_params=_params(("arbitrary", "arbitrary")),
            name=f"ffn_l{l}",
        )(x, g_ffn_pre, wgu, wdown, g_ffn_post)
    return x
```

```python
import functools

import numpy as np
import jax
import jax.numpy as jnp
from jax import lax
from jax.experimental import pallas as pl
from jax.experimental.pallas import tpu as pltpu

D_MODEL = 1024
CHUNK = 64
N_MEM = 256
EPS = 1e-6
GLA_HEADS = 4
GLA_DK = 64
GLA_DV = 128
GLA_WIDTH = GLA_HEADS * GLA_DV
GLA_KW = GLA_HEADS * GLA_DK
GATE_RANK = 16
GATE_TAU = 16.0
SG_WIDTH = D_MODEL - GLA_WIDTH
SG_GROUPS = 4
SG_GROUP_CH = SG_WIDTH // SG_GROUPS
SG_BLOCK = 128
X_HEADS = 4
X_HEAD_DIM = D_MODEL // X_HEADS
D_FF = 2816
OFF_Q = 0
OFF_K = OFF_Q + GLA_KW
OFF_V = OFF_K + GLA_KW
OFF_G = OFF_V + GLA_WIDTH
OFF_A = OFF_G + GLA_WIDTH
OFF_U = OFF_A + GATE_RANK
OFF_SV = OFF_U + SG_WIDTH
D_IN = OFF_SV + SG_WIDTH

LANES = 128
GATE_PAD = LANES
T_A = 0
T_U = T_A + GATE_PAD
T_SV = T_U + SG_WIDTH
D_TAIL = T_SV + SG_WIDTH

TM_MIXER = 1024
TM_ATTN = 2048
TM_FFN = 1024
SBD_SLOTS = 16
SUB = 256
OUT_PIECE = 256
KV_BATCH = 2
VMEM_LIMIT = 56 * 1024 * 1024

F32 = jnp.float32
BF16 = jnp.bfloat16
NT_DIMS = (((1,), (1,)), ((), ()))


def _rms(x, g):
    ms = jnp.mean(x * x, axis=-1, keepdims=True)
    return x * lax.rsqrt(ms + EPS) * g


def _gelu_tanh(x):
    c = np.sqrt(2.0 / np.pi)
    hx = 0.5 * x
    return hx + hx * jnp.tanh(x * (np.float32(c) + np.float32(c * 0.044715) * (x * x)))


def _silu(x):
    return x / (1.0 + jnp.exp2(x * np.float32(-np.log2(np.e))))


def _dot(a, b):
    return jnp.dot(a, b, preferred_element_type=F32)


def _dot_nt(a, b):
    return lax.dot_general(a, b, NT_DIMS, preferred_element_type=F32)


def _mixer_kernel(x_ref, gpre_ref, win_ref, wtail_ref, wgt_ref, bg_ref, mm_ref, onorm_ref,
                  lng_ref, lnb_ref, ws_ref, bsb_ref, wout_ref, gpost_ref, o_ref, s_ref, sbd_ref):
    n_sub = TM_MIXER // SUB
    n_blocks = SUB // LANES

    @pl.when(pl.program_id(1) == 0)
    def _():
        s_ref[...] = jnp.zeros_like(s_ref)
        sbd_ref[...] = jnp.zeros_like(sbd_ref)

    state = [s_ref[GLA_DK * hh:GLA_DK * (hh + 1), :] for hh in range(GLA_HEADS)]
    ctx = [dict() for _ in range(n_sub)]
    row_chunk = lax.broadcasted_iota(jnp.int32, (SG_BLOCK, SG_BLOCK), 0) // CHUNK
    col_chunk = lax.broadcasted_iota(jnp.int32, (SG_BLOCK, SG_BLOCK), 1) // CHUNK
    causal = col_chunk <= row_chunk

    def pre_norm(k):
        c = ctx[k]
        c["xn"] = _rms(x_ref[0, SUB * k:SUB * (k + 1), :], gpre_ref[...]).astype(BF16)

    def proj_qk(k):
        c = ctx[k]
        h = _dot(c["xn"], win_ref[:, OFF_Q:OFF_V])
        c["q"] = (h[:, :GLA_KW] * np.float32(GLA_DK ** -0.5)).astype(BF16)
        c["k_t"] = h[:, GLA_KW:].T

    def proj_v(k):
        c = ctx[k]
        c["v"] = _dot(c["xn"], win_ref[:, OFF_V:OFF_G]).astype(BF16)

    def proj_sv(k):
        c = ctx[k]
        sv = _gelu_tanh(_dot(c["xn"], wtail_ref[:, T_SV:D_TAIL]))
        mu = jnp.mean(sv, axis=-1, keepdims=True)
        svc = sv - mu
        var = jnp.mean(svc * svc, axis=-1, keepdims=True)
        c["svn"] = (svc * lax.rsqrt(var + EPS) * lng_ref[...] + lnb_ref[...]).astype(BF16)

    def proj_g(k):
        c = ctx[k]
        c["gate"] = _silu(_dot(c["xn"], win_ref[:, OFF_G:OFF_A]))

    def proj_au(k):
        c = ctx[k]
        h = _dot(c["xn"], wtail_ref[:, T_A:T_SV])
        c["a_lr"] = h[:, T_A:T_U].astype(BF16)
        c["u"] = _gelu_tanh(h[:, T_U:T_SV])

    def gate_logits(k):
        c = ctx[k]
        z_t = _dot_nt(wgt_ref[...], c["a_lr"]) + jnp.concatenate([bg_ref[...]] * n_blocks, axis=1)
        la_t = (jnp.minimum(z_t, 0.0) - jnp.log(1.0 + jnp.exp(-jnp.abs(z_t)))) * np.float32(1.0 / GATE_TAU)
        c["la_hi"] = la_t.astype(BF16)
        c["la_lo"] = (la_t - c["la_hi"].astype(F32)).astype(BF16)

    def decays(k):
        c = ctx[k]
        c["ex"], c["kd"] = [], []
        for j in range(n_blocks):
            cl = slice(LANES * j, LANES * (j + 1))
            ex = jnp.exp(_dot(jnp.concatenate([c["la_hi"][:, cl], c["la_lo"][:, cl]], axis=1), mm_ref[...]))
            c["ex"].append(ex)
            c["kd"].append((c["k_t"][:, cl] * ex[:, :LANES]).astype(BF16))

    def increments(k):
        c = ctx[k]
        c["incs"] = []
        zeros = jnp.zeros((CHUNK, GLA_DV), BF16)
        for j in range(n_blocks):
            incs = []
            for hh in range(GLA_HEADS):
                rs = slice(GLA_DK * hh, GLA_DK * (hh + 1))
                cs = slice(GLA_DV * hh, GLA_DV * (hh + 1))
                v0 = c["v"][LANES * j:LANES * j + CHUNK, cs]
                v1 = c["v"][LANES * j + CHUNK:LANES * (j + 1), cs]
                v_bd = jnp.concatenate([jnp.concatenate([v0, zeros], axis=1),
                                        jnp.concatenate([zeros, v1], axis=1)], axis=0)
                incs.append(_dot(c["kd"][j][rs], v_bd))
            c["incs"].append(incs)

    def spatial_mix(k):
        c = ctx[k]
        z_cols = []
        for gi in range(SG_GROUPS):
            cs = slice(SG_GROUP_CH * gi, SG_GROUP_CH * (gi + 1))
            w_g = jnp.where(causal, ws_ref[gi], jnp.zeros((), BF16))
            rhs = jnp.concatenate([c["svn"][SG_BLOCK * j:SG_BLOCK * (j + 1), cs] for j in range(n_blocks)], axis=1)
            z = _dot(w_g, rhs)
            z_cols.append([z[:, SG_GROUP_CH * j:SG_GROUP_CH * (j + 1)] + bsb_ref[:, cs] for j in range(n_blocks)])
        z_full = jnp.concatenate(
            [jnp.concatenate([z_cols[gi][j] for gi in range(SG_GROUPS)], axis=1) for j in range(n_blocks)], axis=0)
        c["sg_out"] = (c["u"] * z_full).astype(BF16)

    def recurrence_readout(k):
        c = ctx[k]
        o_chunks = []
        for j in range(n_blocks):
            for c2 in range(2):
                ci = (k * (SUB // CHUNK) + 2 * j + c2) % SBD_SLOTS
                tot = c["ex"][j][:, LANES * (1 + c2):LANES * (2 + c2)]
                for hh in range(GLA_HEADS):
                    rs = slice(GLA_DK * hh, GLA_DK * (hh + 1))
                    cs = slice(GLA_DV * hh, GLA_DV * (hh + 1))
                    state[hh] = tot[rs] * state[hh] + c["incs"][j][hh][:, GLA_DV * c2:GLA_DV * (c2 + 1)]
                    sbd_ref[ci, rs, cs] = state[hh].astype(BF16)
                rows = slice(LANES * j + CHUNK * c2, LANES * j + CHUNK * (c2 + 1))
                o_chunks.append(_dot(c["q"][rows], sbd_ref[ci]))
        c["o"] = jnp.concatenate(o_chunks, axis=0)

    def mix_input(k):
        c = ctx[k]
        heads = []
        for hh in range(GLA_HEADS):
            oh = c["o"][:, GLA_DV * hh:GLA_DV * (hh + 1)]
            heads.append(oh * lax.rsqrt(jnp.mean(oh * oh, axis=-1, keepdims=True) + EPS))
        gla_out = (jnp.concatenate(heads, axis=-1) * onorm_ref[...] * c["gate"]).astype(BF16)
        c["lhs"] = jnp.concatenate([gla_out, c["sg_out"]], axis=1)
        c["mix"] = []

    def out_mm(k, pieces):
        c = ctx[k]
        for p in pieces:
            c["mix"].append(_dot(c["lhs"], wout_ref[:, OUT_PIECE * p:OUT_PIECE * (p + 1)]))

    def epilogue(k):
        c = ctx[k]
        mix = jnp.concatenate(c["mix"], axis=1)
        rows = slice(SUB * k, SUB * (k + 1))
        o_ref[0, rows, :] = x_ref[0, rows, :] + _rms(mix, gpost_ref[...])
        c.clear()

    pre_norm(0)
    proj_qk(0)
    proj_v(0)
    proj_sv(0)
    proj_g(0)
    proj_au(0)
    all_pieces = tuple(range(D_MODEL // OUT_PIECE))
    for k in range(n_sub):
        nxt = k + 1 < n_sub
        last = n_sub > 1 and not nxt
        gate_logits(k)
        if nxt:
            pre_norm(k + 1)
            proj_qk(k + 1)
        if last:
            out_mm(k - 1, all_pieces[:2])
        decays(k)
        if nxt:
            proj_v(k + 1)
        if last:
            out_mm(k - 1, all_pieces[2:3])
        increments(k)
        if nxt:
            proj_sv(k + 1)
        if last:
            out_mm(k - 1, all_pieces[3:])
            epilogue(k - 1)
        spatial_mix(k)
        recurrence_readout(k)
        if nxt:
            proj_g(k + 1)
            proj_au(k + 1)
        mix_input(k)
        if k + 2 == n_sub:
            continue
        out_mm(k, all_pieces)
        epilogue(k)

    for hh in range(GLA_HEADS):
        s_ref[GLA_DK * hh:GLA_DK * (hh + 1), :] = state[hh]


def _kv_kernel(mem_ref, gmem_ref, wkv_ref, k_ref, v_ref):
    nb = mem_ref.shape[0]
    memn = _rms(mem_ref[...].reshape(nb * N_MEM, D_MODEL), gmem_ref[...]).astype(BF16)
    kv = _dot(memn, wkv_ref[...])
    k_ref[...] = kv[:, :D_MODEL].astype(BF16).reshape(nb, N_MEM, D_MODEL)
    v_ref[...] = kv[:, D_MODEL:].astype(BF16).reshape(nb, N_MEM, D_MODEL)


def _attn_kernel(x_ref, gpre_ref, wq_ref, k_ref, v_ref, wo_ref, gpost_ref, o_ref):
    n_sub = TM_ATTN // SUB
    ctx = [dict() for _ in range(n_sub)]

    def q_proj(k):
        c = ctx[k]
        xn = _rms(x_ref[0, SUB * k:SUB * (k + 1), :], gpre_ref[...]).astype(BF16)
        c["q"] = _dot(xn, wq_ref[...]).astype(BF16)

    def scores(k):
        c = ctx[k]
        c["p"], c["denom"] = [], []
        for hh in range(X_HEADS):
            cs = slice(X_HEAD_DIM * hh, X_HEAD_DIM * (hh + 1))
            sc = _dot_nt(c["q"][:, cs], k_ref[0, :, cs])
            p = jnp.exp(sc - jnp.max(sc, axis=-1, keepdims=True))
            c["denom"].append(jnp.sum(p, axis=-1, keepdims=True))
            c["p"].append(p.astype(BF16))

    def values(k):
        c = ctx[k]
        outs = []
        for hh in range(X_HEADS):
            cs = slice(X_HEAD_DIM * hh, X_HEAD_DIM * (hh + 1))
            outs.append((_dot(c["p"][hh], v_ref[0, :, cs]) / c["denom"][hh]).astype(BF16))
        c["o"] = jnp.concatenate(outs, axis=-1)

    def out_mm(k, pieces):
        c = ctx[k]
        c.setdefault("xa", [])
        for p in pieces:
            c["xa"].append(_dot(c["o"], wo_ref[:, OUT_PIECE * p:OUT_PIECE * (p + 1)]))

    def epilogue(k):
        c = ctx[k]
        xa = jnp.concatenate(c["xa"], axis=1)
        rows = slice(SUB * k, SUB * (k + 1))
        o_ref[0, rows, :] = x_ref[0, rows, :] + _rms(xa, gpost_ref[...])
        c.clear()

    all_pieces = tuple(range(D_MODEL // OUT_PIECE))
    q_proj(0)
    scores(0)
    for k in range(n_sub):
        nxt = k + 1 < n_sub
        last = n_sub > 1 and not nxt
        if nxt:
            q_proj(k + 1)
        if last:
            out_mm(k - 1, all_pieces[:2])
        values(k)
        if nxt:
            scores(k + 1)
        if last:
            out_mm(k - 1, all_pieces[2:])
            epilogue(k - 1)
        if k + 2 == n_sub:
            continue
        out_mm(k, all_pieces)
        epilogue(k)


def _ffn_kernel(x_ref, gpre_ref, wgu_ref, wdown_ref, gpost_ref, o_ref):
    n_sub = TM_FFN // SUB
    ctx = [dict() for _ in range(n_sub)]

    def up_proj(k):
        c = ctx[k]
        xn = _rms(x_ref[0, SUB * k:SUB * (k + 1), :], gpre_ref[...]).astype(BF16)
        hg = _dot(xn, wgu_ref[:, :D_FF])
        hu = _dot(xn, wgu_ref[:, D_FF:])
        c["act"] = (_silu(hg) * hu).astype(BF16)

    def down_proj(k):
        c = ctx[k]
        f = _dot(c["act"], wdown_ref[...])
        rows = slice(SUB * k, SUB * (k + 1))
        o_ref[0, rows, :] = x_ref[0, rows, :] + _rms(f, gpost_ref[...])
        c.clear()

    up_proj(0)
    for k in range(n_sub):
        if k + 1 < n_sub:
            up_proj(k + 1)
        down_proj(k)


def _layer_spec(shape, layer):
    zeros = (0,) * len(shape)
    return pl.BlockSpec((None,) + tuple(shape), lambda b, t: (layer,) + zeros,
                        pipeline_mode=pl.Buffered(1))


def _const_spec(shape):
    zeros = (0,) * len(shape)
    return pl.BlockSpec(tuple(shape), lambda b, t: zeros, pipeline_mode=pl.Buffered(1))


def _tile_spec(tm):
    return pl.BlockSpec((1, tm, D_MODEL), lambda b, t: (b, t, 0))


def _params(semantics):
    return pltpu.CompilerParams(dimension_semantics=semantics, vmem_limit_bytes=VMEM_LIMIT)


def _suffix_sum_matrix():
    r = np.arange(LANES)[:, None]
    s = np.arange(LANES)[None, :]
    suffix = (r > s) & (r // CHUNK == s // CHUNK)
    tot0 = np.broadcast_to(r // CHUNK == 0, (LANES, LANES))
    tot1 = np.broadcast_to(r // CHUNK == 1, (LANES, LANES))
    m = np.concatenate([suffix, tot0, tot1], axis=1).astype(np.float32)
    return np.concatenate([m, m], axis=0)


def kernel(x, mem, norm_mix_pre, w_in, w_gate_up, b_gate, gla_onorm, sg_ln_g, sg_ln_b, w_spatial,
           b_spatial, w_out, norm_mix_post, norm_x_pre, norm_mem, w_xq, w_xkv, w_xo, norm_x_post,
           norm_ffn_pre, w_ffn_gu, w_ffn_down, norm_ffn_post):
    batch, seq, _ = x.shape
    depth = w_in.shape[0]
    assert SUB % SG_BLOCK == 0 and SG_BLOCK == 2 * CHUNK
    for tm in (TM_MIXER, TM_ATTN, TM_FFN):
        assert seq % tm == 0 and tm % SUB == 0

    w_head = w_in[:, :, OFF_Q:OFF_A].astype(BF16)
    w_tail = jnp.concatenate(
        [w_in[:, :, OFF_A:OFF_U].astype(BF16), jnp.zeros((depth, D_MODEL, GATE_PAD - GATE_RANK), BF16),
         w_in[:, :, OFF_U:D_IN].astype(BF16)], axis=-1)
    wg_t = jnp.pad(jnp.swapaxes(w_gate_up, 1, 2),
                   ((0, 0), (0, 0), (0, GATE_PAD - GATE_RANK))).astype(BF16)
    bg = jnp.broadcast_to(b_gate[:, :, None], (depth, GLA_KW, LANES)).astype(F32)
    mm = jnp.asarray(_suffix_sum_matrix(), BF16)
    ws = w_spatial.astype(BF16)
    bsb = jnp.repeat(jnp.swapaxes(b_spatial, 1, 2), SG_GROUP_CH, axis=2).astype(F32)
    wout = w_out.astype(BF16)
    wq = (w_xq * np.float32(X_HEAD_DIM ** -0.5)).astype(BF16)
    wkv = w_xkv.astype(BF16)
    wo = w_xo.astype(BF16)
    wgu = w_ffn_gu.astype(BF16)
    wdown = w_ffn_down.astype(BF16)

    def vec(a):
        return a.reshape(depth, 1, a.shape[-1]).astype(F32)

    g_mix_pre, g_mix_post = vec(norm_mix_pre), vec(norm_mix_post)
    g_x_pre, g_x_post, g_mem = vec(norm_x_pre), vec(norm_x_post), vec(norm_mem)
    g_ffn_pre, g_ffn_post = vec(norm_ffn_pre), vec(norm_ffn_post)
    onorm, lng, lnb = vec(gla_onorm), vec(sg_ln_g), vec(sg_ln_b)

    x_shape = jax.ShapeDtypeStruct(x.shape, F32)
    kv_shape = jax.ShapeDtypeStruct((depth, batch, N_MEM, D_MODEL), BF16)

    assert batch % KV_BATCH == 0
    kv_block = pl.BlockSpec((None, KV_BATCH, N_MEM, D_MODEL), lambda l, b: (l, b, 0, 0))
    k_mem, v_mem = pl.pallas_call(
        _kv_kernel,
        grid=(depth, batch // KV_BATCH),
        in_specs=[pl.BlockSpec((KV_BATCH, N_MEM, D_MODEL), lambda l, b: (b, 0, 0)),
                  pl.BlockSpec((None, 1, D_MODEL), lambda l, b: (l, 0, 0)),
                  pl.BlockSpec((None, D_MODEL, 2 * D_MODEL), lambda l, b: (l, 0, 0))],
        out_specs=[kv_block, kv_block],
        out_shape=[kv_shape, kv_shape],
        compiler_params=_params(("arbitrary", "arbitrary")),
        name="memkv",
    )(mem, g_mem, wkv)

    for l in range(depth):
        lspec = functools.partial(_layer_spec, layer=l)
        x = pl.pallas_call(
            _mixer_kernel,
            grid=(batch, seq // TM_MIXER),
            in_specs=[_tile_spec(TM_MIXER), lspec((1, D_MODEL)), lspec((D_MODEL, OFF_A)),
                      lspec((D_MODEL, D_TAIL)),
                      lspec((GLA_KW, GATE_PAD)), lspec((GLA_KW, LANES)),
                      _const_spec((2 * LANES, 3 * LANES)), lspec((1, GLA_WIDTH)), lspec((1, SG_WIDTH)),
                      lspec((1, SG_WIDTH)), lspec((SG_GROUPS, SG_BLOCK, SG_BLOCK)),
                      lspec((SG_BLOCK, SG_WIDTH)), lspec((D_MODEL, D_MODEL)), lspec((1, D_MODEL))],
            out_specs=_tile_spec(TM_MIXER),
            out_shape=x_shape,
            scratch_shapes=[pltpu.VMEM((GLA_KW, GLA_DV), F32),
                            pltpu.VMEM((SBD_SLOTS, GLA_KW, GLA_WIDTH), BF16)],
            compiler_params=_params(("arbitrary", "arbitrary")),
            name=f"mixer_l{l}",
        )(x, g_mix_pre, w_head, w_tail, wg_t, bg, mm, onorm, lng, lnb, ws, bsb, wout, g_mix_post)

        x = pl.pallas_call(
            _attn_kernel,
            grid=(batch, seq // TM_ATTN),
            in_specs=[_tile_spec(TM_ATTN), lspec((1, D_MODEL)), lspec((D_MODEL, D_MODEL)),
                      pl.BlockSpec((None, 1, N_MEM, D_MODEL), lambda b, t: (l, b, 0, 0)),
                      pl.BlockSpec((None, 1, N_MEM, D_MODEL), lambda b, t: (l, b, 0, 0)),
                      lspec((D_MODEL, D_MODEL)), lspec((1, D_MODEL))],
            out_specs=_tile_spec(TM_ATTN),
            out_shape=x_shape,
            compiler_params=_params(("arbitrary", "arbitrary")),
            name=f"xattn_l{l}",
        )(x, g_x_pre, wq, k_mem, v_mem, wo, g_x_post)

        x = pl.pallas_call(
            _ffn_kernel,
            grid=(batch, seq // TM_FFN),
            in_specs=[_tile_spec(TM_FFN), lspec((1, D_MODEL)), lspec((D_MODEL, 2 * D_FF)),
                      lspec((D_FF, D_MODEL)), lspec((1, D_MODEL))],
            out_specs=_tile_spec(TM_FFN),
            out_shape=x_shape,
            compiler_params=_params(("arbitrary", "arbitrary")),
            name=f"ffn_l{l}",
        )(x, g_ffn_pre, wgu, wdown, g_ffn_post)
    return x
```

```python
import functools

import numpy as np
import jax
import jax.numpy as jnp
from jax import lax
from jax.experimental import pallas as pl
from jax.experimental.pallas import tpu as pltpu

D_MODEL = 1024
CHUNK = 64
N_MEM = 256
EPS = 1e-6
GLA_HEADS = 4
GLA_DK = 64
GLA_DV = 128
GLA_WIDTH = GLA_HEADS * GLA_DV
GLA_KW = GLA_HEADS * GLA_DK
GATE_RANK = 16
GATE_TAU = 16.0
SG_WIDTH = D_MODEL - GLA_WIDTH
SG_GROUPS = 4
SG_GROUP_CH = SG_WIDTH // SG_GROUPS
SG_BLOCK = 128
X_HEADS = 4
X_HEAD_DIM = D_MODEL // X_HEADS
D_FF = 2816
OFF_Q = 0
OFF_K = OFF_Q + GLA_KW
OFF_V = OFF_K + GLA_KW
OFF_G = OFF_V + GLA_WIDTH
OFF_A = OFF_G + GLA_WIDTH
OFF_U = OFF_A + GATE_RANK
OFF_SV = OFF_U + SG_WIDTH
D_IN = OFF_SV + SG_WIDTH

LANES = 128
GATE_PAD = LANES
T_A = 0
T_U = T_A + GATE_PAD
T_SV = T_U + SG_WIDTH
D_TAIL = T_SV + SG_WIDTH

TM_MIXER = 1024
TM_ATTN = 2048
TM_FFN = 1024
SBD_SLOTS = 16
SUB = 256
OUT_PIECE = 256
KV_BATCH = 2
VMEM_LIMIT = 56 * 1024 * 1024

F32 = jnp.float32
BF16 = jnp.bfloat16
NT_DIMS = (((1,), (1,)), ((), ()))
LOG2E = float(np.log2(np.e))


def _rms(x, g):
    ms = jnp.mean(x * x, axis=-1, keepdims=True)
    return x * lax.rsqrt(ms + EPS) * g


def _gelu_tanh(x):
    c = np.sqrt(2.0 / np.pi)
    hx = 0.5 * x
    return hx + hx * jnp.tanh(x * (np.float32(c) + np.float32(c * 0.044715) * (x * x)))


def _silu(x):
    return x / (1.0 + jnp.exp2(x * np.float32(-LOG2E)))


def _dot(a, b):
    return jnp.dot(a, b, preferred_element_type=F32)


def _dot_nt(a, b):
    return lax.dot_general(a, b, NT_DIMS, preferred_element_type=F32)


def _mixer_kernel(x_ref, gpre_ref, win_ref, wtail_ref, wgt_ref, bg_ref, mm_ref, onorm_ref,
                  lng_ref, lnb_ref, ws_ref, bsb_ref, wout_ref, gpost_ref, o_ref, s_ref, sbd_ref):
    n_sub = TM_MIXER // SUB
    n_blocks = SUB // LANES

    @pl.when(pl.program_id(1) == 0)
    def _():
        s_ref[...] = jnp.zeros_like(s_ref)
        sbd_ref[...] = jnp.zeros_like(sbd_ref)

    state = [s_ref[GLA_DK * hh:GLA_DK * (hh + 1), :] for hh in range(GLA_HEADS)]
    ctx = [dict() for _ in range(n_sub)]
    row_chunk = lax.broadcasted_iota(jnp.int32, (SG_BLOCK, SG_BLOCK), 0) // CHUNK
    col_chunk = lax.broadcasted_iota(jnp.int32, (SG_BLOCK, SG_BLOCK), 1) // CHUNK
    causal = col_chunk <= row_chunk

    def pre_norm(k):
        c = ctx[k]
        c["xn"] = _rms(x_ref[0, SUB * k:SUB * (k + 1), :], gpre_ref[...]).astype(BF16)

    def proj_qk(k):
        c = ctx[k]
        h = _dot(c["xn"], win_ref[:, OFF_Q:OFF_V])
        c["q"] = (h[:, :GLA_KW] * np.float32(GLA_DK ** -0.5)).astype(BF16)
        c["k_t"] = h[:, GLA_KW:].T

    def proj_v(k):
        c = ctx[k]
        c["v"] = _dot(c["xn"], win_ref[:, OFF_V:OFF_G]).astype(BF16)

    def proj_sv(k):
        c = ctx[k]
        sv = _gelu_tanh(_dot(c["xn"], wtail_ref[:, T_SV:D_TAIL]))
        mu = jnp.mean(sv, axis=-1, keepdims=True)
        svc = sv - mu
        var = jnp.mean(svc * svc, axis=-1, keepdims=True)
        c["svn"] = (svc * lax.rsqrt(var + EPS) * lng_ref[...] + lnb_ref[...]).astype(BF16)

    def proj_g(k):
        c = ctx[k]
        c["gate"] = _silu(_dot(c["xn"], win_ref[:, OFF_G:OFF_A]))

    def proj_au(k):
        c = ctx[k]
        h = _dot(c["xn"], wtail_ref[:, T_A:T_SV])
        c["a_lr"] = h[:, T_A:T_U].astype(BF16)
        c["u"] = _gelu_tanh(h[:, T_U:T_SV])

    def gate_logits(k):
        c = ctx[k]
        z_t = _dot_nt(wgt_ref[...], c["a_lr"]) + jnp.concatenate([bg_ref[...]] * n_blocks, axis=1)
        la_t = ((jnp.minimum(z_t, 0.0) - jnp.log(1.0 + jnp.exp2(jnp.abs(z_t) * np.float32(-LOG2E))))
                * np.float32(LOG2E / GATE_TAU))
        c["la_hi"] = la_t.astype(BF16)
        c["la_lo"] = (la_t - c["la_hi"].astype(F32)).astype(BF16)

    def decays(k):
        c = ctx[k]
        c["ex"], c["kd"] = [], []
        for j in range(n_blocks):
            cl = slice(LANES * j, LANES * (j + 1))
            ex = jnp.exp2(_dot(jnp.concatenate([c["la_hi"][:, cl], c["la_lo"][:, cl]], axis=1), mm_ref[...]))
            c["ex"].append(ex)
            c["kd"].append((c["k_t"][:, cl] * ex[:, :LANES]).astype(BF16))

    def increments(k):
        c = ctx[k]
        c["incs"] = []
        zeros = jnp.zeros((CHUNK, GLA_DV), BF16)
        for j in range(n_blocks):
            incs = []
            for hh in range(GLA_HEADS):
                rs = slice(GLA_DK * hh, GLA_DK * (hh + 1))
                cs = slice(GLA_DV * hh, GLA_DV * (hh + 1))
                v0 = c["v"][LANES * j:LANES * j + CHUNK, cs]
                v1 = c["v"][LANES * j + CHUNK:LANES * (j + 1), cs]
                v_bd = jnp.concatenate([jnp.concatenate([v0, zeros], axis=1),
                                        jnp.concatenate([zeros, v1], axis=1)], axis=0)
                incs.append(_dot(c["kd"][j][rs], v_bd))
            c["incs"].append(incs)

    def spatial_mix(k):
        c = ctx[k]
        z_cols = []
        for gi in range(SG_GROUPS):
            cs = slice(SG_GROUP_CH * gi, SG_GROUP_CH * (gi + 1))
            w_g = jnp.where(causal, ws_ref[gi], jnp.zeros((), BF16))
            rhs = jnp.concatenate([c["svn"][SG_BLOCK * j:SG_BLOCK * (j + 1), cs] for j in range(n_blocks)], axis=1)
            z = _dot(w_g, rhs)
            z_cols.append([z[:, SG_GROUP_CH * j:SG_GROUP_CH * (j + 1)] + bsb_ref[:, cs] for j in range(n_blocks)])
        z_full = jnp.concatenate(
            [jnp.concatenate([z_cols[gi][j] for gi in range(SG_GROUPS)], axis=1) for j in range(n_blocks)], axis=0)
        c["sg_out"] = (c["u"] * z_full).astype(BF16)

    def recurrence_readout(k):
        c = ctx[k]
        o_chunks = []
        for j in range(n_blocks):
            for c2 in range(2):
                ci = (k * (SUB // CHUNK) + 2 * j + c2) % SBD_SLOTS
                tot = c["ex"][j][:, LANES * (1 + c2):LANES * (2 + c2)]
                for hh in range(GLA_HEADS):
                    rs = slice(GLA_DK * hh, GLA_DK * (hh + 1))
                    cs = slice(GLA_DV * hh, GLA_DV * (hh + 1))
                    state[hh] = tot[rs] * state[hh] + c["incs"][j][hh][:, GLA_DV * c2:GLA_DV * (c2 + 1)]
                    sbd_ref[ci, rs, cs] = state[hh].astype(BF16)
                rows = slice(LANES * j + CHUNK * c2, LANES * j + CHUNK * (c2 + 1))
                o_chunks.append(_dot(c["q"][rows], sbd_ref[ci]))
        c["o"] = jnp.concatenate(o_chunks, axis=0)

    def mix_input(k):
        c = ctx[k]
        heads = []
        for hh in range(GLA_HEADS):
            oh = c["o"][:, GLA_DV * hh:GLA_DV * (hh + 1)]
            heads.append(oh * lax.rsqrt(jnp.mean(oh * oh, axis=-1, keepdims=True) + EPS))
        gla_out = (jnp.concatenate(heads, axis=-1) * onorm_ref[...] * c["gate"]).astype(BF16)
        c["lhs"] = jnp.concatenate([gla_out, c["sg_out"]], axis=1)
        c["mix"] = []

    def out_mm(k, pieces):
        c = ctx[k]
        for p in pieces:
            c["mix"].append(_dot(c["lhs"], wout_ref[:, OUT_PIECE * p:OUT_PIECE * (p + 1)]))

    def epilogue(k):
        c = ctx[k]
        mix = jnp.concatenate(c["mix"], axis=1)
        rows = slice(SUB * k, SUB * (k + 1))
        o_ref[0, rows, :] = x_ref[0, rows, :] + _rms(mix, gpost_ref[...])
        c.clear()

    pre_norm(0)
    proj_qk(0)
    proj_v(0)
    proj_sv(0)
    proj_g(0)
    proj_au(0)
    all_pieces = tuple(range(D_MODEL // OUT_PIECE))
    for k in range(n_sub):
        nxt = k + 1 < n_sub
        last = n_sub > 1 and not nxt
        gate_logits(k)
        if nxt:
            pre_norm(k + 1)
            proj_qk(k + 1)
        if last:
            out_mm(k - 1, all_pieces[:2])
        decays(k)
        if nxt:
            proj_v(k + 1)
        if last:
            out_mm(k - 1, all_pieces[2:3])
        increments(k)
        if nxt:
            proj_sv(k + 1)
        if last:
            out_mm(k - 1, all_pieces[3:])
            epilogue(k - 1)
        spatial_mix(k)
        recurrence_readout(k)
        if nxt:
            proj_g(k + 1)
            proj_au(k + 1)
        mix_input(k)
        if k + 2 == n_sub:
            continue
        out_mm(k, all_pieces)
        epilogue(k)

    for hh in range(GLA_HEADS):
        s_ref[GLA_DK * hh:GLA_DK * (hh + 1), :] = state[hh]


def _kv_kernel(mem_ref, gmem_ref, wkv_ref, k_ref, v_ref):
    nb = mem_ref.shape[0]
    memn = _rms(mem_ref[...].reshape(nb * N_MEM, D_MODEL), gmem_ref[...]).astype(BF16)
    kv = _dot(memn, wkv_ref[...])
    k_ref[...] = kv[:, :D_MODEL].astype(BF16).reshape(nb, N_MEM, D_MODEL)
    v_ref[...] = kv[:, D_MODEL:].astype(BF16).reshape(nb, N_MEM, D_MODEL)


def _attn_kernel(x_ref, gpre_ref, wq_ref, k_ref, v_ref, wo_ref, gpost_ref, o_ref):
    n_sub = TM_ATTN // SUB
    ctx = [dict() for _ in range(n_sub)]

    def q_proj(k):
        c = ctx[k]
        xn = _rms(x_ref[0, SUB * k:SUB * (k + 1), :], gpre_ref[...]).astype(BF16)
        c["q"] = _dot(xn, wq_ref[...]).astype(BF16)

    def scores(k):
        c = ctx[k]
        c["p"], c["denom"] = [], []
        for hh in range(X_HEADS):
            cs = slice(X_HEAD_DIM * hh, X_HEAD_DIM * (hh + 1))
            sc = _dot_nt(c["q"][:, cs], k_ref[0, :, cs])
            p = jnp.exp2(sc - jnp.max(sc, axis=-1, keepdims=True))
            c["denom"].append(jnp.sum(p, axis=-1, keepdims=True))
            c["p"].append(p.astype(BF16))

    def values(k):
        c = ctx[k]
        outs = []
        for hh in range(X_HEADS):
            cs = slice(X_HEAD_DIM * hh, X_HEAD_DIM * (hh + 1))
            outs.append((_dot(c["p"][hh], v_ref[0, :, cs]) * (1.0 / c["denom"][hh])).astype(BF16))
        c["o"] = jnp.concatenate(outs, axis=-1)

    def out_mm(k, pieces):
        c = ctx[k]
        c.setdefault("xa", [])
        for p in pieces:
            c["xa"].append(_dot(c["o"], wo_ref[:, OUT_PIECE * p:OUT_PIECE * (p + 1)]))

    def epilogue(k):
        c = ctx[k]
        xa = jnp.concatenate(c["xa"], axis=1)
        rows = slice(SUB * k, SUB * (k + 1))
        o_ref[0, rows, :] = x_ref[0, rows, :] + _rms(xa, gpost_ref[...])
        c.clear()

    all_pieces = tuple(range(D_MODEL // OUT_PIECE))
    q_proj(0)
    scores(0)
    for k in range(n_sub):
        nxt = k + 1 < n_sub
        last = n_sub > 1 and not nxt
        if nxt:
            q_proj(k + 1)
        if last:
            out_mm(k - 1, all_pieces[:2])
        values(k)
        if nxt:
            scores(k + 1)
        if last:
            out_mm(k - 1, all_pieces[2:])
            epilogue(k - 1)
        if k + 2 == n_sub:
            continue
        out_mm(k, all_pieces)
        epilogue(k)


def _ffn_kernel(x_ref, gpre_ref, wgu_ref, wdown_ref, gpost_ref, o_ref):
    n_sub = TM_FFN // SUB
    ctx = [dict() for _ in range(n_sub)]

    def up_proj(k):
        c = ctx[k]
        xn = _rms(x_ref[0, SUB * k:SUB * (k + 1), :], gpre_ref[...]).astype(BF16)
        hg = _dot(xn, wgu_ref[:, :D_FF])
        hu = _dot(xn, wgu_ref[:, D_FF:])
        c["act"] = (_silu(hg) * hu).astype(BF16)

    def down_proj(k):
        c = ctx[k]
        f = _dot(c["act"], wdown_ref[...])
        rows = slice(SUB * k, SUB * (k + 1))
        o_ref[0, rows, :] = x_ref[0, rows, :] + _rms(f, gpost_ref[...])
        c.clear()

    up_proj(0)
    for k in range(n_sub):
        if k + 1 < n_sub:
            up_proj(k + 1)
        down_proj(k)


def _layer_spec(shape, layer):
    zeros = (0,) * len(shape)
    return pl.BlockSpec((None,) + tuple(shape), lambda b, t: (layer,) + zeros,
                        pipeline_mode=pl.Buffered(1))


def _const_spec(shape):
    zeros = (0,) * len(shape)
    return pl.BlockSpec(tuple(shape), lambda b, t: zeros, pipeline_mode=pl.Buffered(1))


def _tile_spec(tm):
    return pl.BlockSpec((1, tm, D_MODEL), lambda b, t: (b, t, 0))


def _params(semantics):
    return pltpu.CompilerParams(dimension_semantics=semantics, vmem_limit_bytes=VMEM_LIMIT)


def _suffix_sum_matrix():
    r = np.arange(LANES)[:, None]
    s = np.arange(LANES)[None, :]
    suffix = (r > s) & (r // CHUNK == s // CHUNK)
    tot0 = np.broadcast_to(r // CHUNK == 0, (LANES, LANES))
    tot1 = np.broadcast_to(r // CHUNK == 1, (LANES, LANES))
    m = np.concatenate([suffix, tot0, tot1], axis=1).astype(np.float32)
    return np.concatenate([m, m], axis=0)


def kernel(x, mem, norm_mix_pre, w_in, w_gate_up, b_gate, gla_onorm, sg_ln_g, sg_ln_b, w_spatial,
           b_spatial, w_out, norm_mix_post, norm_x_pre, norm_mem, w_xq, w_xkv, w_xo, norm_x_post,
           norm_ffn_pre, w_ffn_gu, w_ffn_down, norm_ffn_post):
    batch, seq, _ = x.shape
    depth = w_in.shape[0]
    assert SUB % SG_BLOCK == 0 and SG_BLOCK == 2 * CHUNK
    for tm in (TM_MIXER, TM_ATTN, TM_FFN):
        assert seq % tm == 0 and tm % SUB == 0

    w_head = w_in[:, :, OFF_Q:OFF_A].astype(BF16)
    w_tail = jnp.concatenate(
        [w_in[:, :, OFF_A:OFF_U].astype(BF16), jnp.zeros((depth, D_MODEL, GATE_PAD - GATE_RANK), BF16),
         w_in[:, :, OFF_U:D_IN].astype(BF16)], axis=-1)
    wg_t = jnp.pad(jnp.swapaxes(w_gate_up, 1, 2),
                   ((0, 0), (0, 0), (0, GATE_PAD - GATE_RANK))).astype(BF16)
    bg = jnp.broadcast_to(b_gate[:, :, None], (depth, GLA_KW, LANES)).astype(F32)
    mm = jnp.asarray(_suffix_sum_matrix(), BF16)
    ws = w_spatial.astype(BF16)
    bsb = jnp.repeat(jnp.swapaxes(b_spatial, 1, 2), SG_GROUP_CH, axis=2).astype(F32)
    wout = w_out.astype(BF16)
    wq = (w_xq * np.float32(X_HEAD_DIM ** -0.5 * LOG2E)).astype(BF16)
    wkv = w_xkv.astype(BF16)
    wo = w_xo.astype(BF16)
    wgu = w_ffn_gu.astype(BF16)
    wdown = w_ffn_down.astype(BF16)

    def vec(a):
        return a.reshape(depth, 1, a.shape[-1]).astype(F32)

    g_mix_pre, g_mix_post = vec(norm_mix_pre), vec(norm_mix_post)
    g_x_pre, g_x_post, g_mem = vec(norm_x_pre), vec(norm_x_post), vec(norm_mem)
    g_ffn_pre, g_ffn_post = vec(norm_ffn_pre), vec(norm_ffn_post)
    onorm, lng, lnb = vec(gla_onorm), vec(sg_ln_g), vec(sg_ln_b)

    x_shape = jax.ShapeDtypeStruct(x.shape, F32)
    kv_shape = jax.ShapeDtypeStruct((depth, batch, N_MEM, D_MODEL), BF16)

    assert batch % KV_BATCH == 0
    kv_block = pl.BlockSpec((None, KV_BATCH, N_MEM, D_MODEL), lambda l, b: (l, b, 0, 0))
    k_mem, v_mem = pl.pallas_call(
        _kv_kernel,
        grid=(depth, batch // KV_BATCH),
        in_specs=[pl.BlockSpec((KV_BATCH, N_MEM, D_MODEL), lambda l, b: (b, 0, 0)),
                  pl.BlockSpec((None, 1, D_MODEL), lambda l, b: (l, 0, 0)),
                  pl.BlockSpec((None, D_MODEL, 2 * D_MODEL), lambda l, b: (l, 0, 0))],
        out_specs=[kv_block, kv_block],
        out_shape=[kv_shape, kv_shape],
        compiler_params=_params(("arbitrary", "arbitrary")),
        name="memkv",
    )(mem, g_mem, wkv)

    for l in range(depth):
        lspec = functools.partial(_layer_spec, layer=l)
        x = pl.pallas_call(
            _mixer_kernel,
            grid=(batch, seq // TM_MIXER),
            in_specs=[_tile_spec(TM_MIXER), lspec((1, D_MODEL)), lspec((D_MODEL, OFF_A)),
                      lspec((D_MODEL, D_TAIL)),
                      lspec((GLA_KW, GATE_PAD)), lspec((GLA_KW, LANES)),
                      _const_spec((2 * LANES, 3 * LANES)), lspec((1, GLA_WIDTH)), lspec((1, SG_WIDTH)),
                      lspec((1, SG_WIDTH)), lspec((SG_GROUPS, SG_BLOCK, SG_BLOCK)),
                      lspec((SG_BLOCK, SG_WIDTH)), lspec((D_MODEL, D_MODEL)), lspec((1, D_MODEL))],
            out_specs=_tile_spec(TM_MIXER),
            out_shape=x_shape,
            scratch_shapes=[pltpu.VMEM((GLA_KW, GLA_DV), F32),
                            pltpu.VMEM((SBD_SLOTS, GLA_KW, GLA_WIDTH), BF16)],
            compiler_params=_params(("arbitrary", "arbitrary")),
            name=f"mixer_l{l}",
        )(x, g_mix_pre, w_head, w_tail, wg_t, bg, mm, onorm, lng, lnb, ws, bsb, wout, g_mix_post)

        x = pl.pallas_call(
            _attn_kernel,
            grid=(batch, seq // TM_ATTN),
            in_specs=[_tile_spec(TM_ATTN), lspec((1, D_MODEL)), lspec((D_MODEL, D_MODEL)),
                      pl.BlockSpec((None, 1, N_MEM, D_MODEL), lambda b, t: (l, b, 0, 0)),
                      pl.BlockSpec((None, 1, N_MEM, D_MODEL), lambda b, t: (l, b, 0, 0)),
                      lspec((D_MODEL, D_MODEL)), lspec((1, D_MODEL))],
            out_specs=_tile_spec(TM_ATTN),
            out_shape=x_shape,
            compiler_params=_params(("arbitrary", "arbitrary")),
            name=f"xattn_l{l}",
        )(x, g_x_pre, wq, k_mem, v_mem, wo, g_x_post)

        x = pl.pallas_call(
            _ffn_kernel,
            grid=(batch, seq // TM_FFN),
            in_specs=[_tile_spec(TM_FFN), lspec((1, D_MODEL)), lspec((D_MODEL, 2 * D_FF)),
                      lspec((D_FF, D_MODEL)), lspec((1, D_MODEL))],
            out_specs=_tile_spec(TM_FFN),
            out_shape=x_shape,
            compiler_params=_params(("arbitrary", "arbitrary")),
            name=f"ffn_l{l}",
        )(x, g_ffn_pre, wgu, wdown, g_ffn_post)
    return x
```

```python
import functools

import numpy as np
import jax
import jax.numpy as jnp
from jax import lax
from jax.experimental import pallas as pl
from jax.experimental.pallas import tpu as pltpu

D_MODEL = 1024
CHUNK = 64
N_MEM = 256
EPS = 1e-6
GLA_HEADS = 4
GLA_DK = 64
GLA_DV = 128
GLA_WIDTH = GLA_HEADS * GLA_DV
GLA_KW = GLA_HEADS * GLA_DK
GATE_RANK = 16
GATE_TAU = 16.0
SG_WIDTH = D_MODEL - GLA_WIDTH
SG_GROUPS = 4
SG_GROUP_CH = SG_WIDTH // SG_GROUPS
SG_BLOCK = 128
X_HEADS = 4
X_HEAD_DIM = D_MODEL // X_HEADS
D_FF = 2816
OFF_Q = 0
OFF_K = OFF_Q + GLA_KW
OFF_V = OFF_K + GLA_KW
OFF_G = OFF_V + GLA_WIDTH
OFF_A = OFF_G + GLA_WIDTH
OFF_U = OFF_A + GATE_RANK
OFF_SV = OFF_U + SG_WIDTH
D_IN = OFF_SV + SG_WIDTH

LANES = 128
GATE_PAD = LANES
T_A = 0
T_U = T_A + GATE_PAD
T_SV = T_U + SG_WIDTH
D_TAIL = T_SV + SG_WIDTH

TM_MIXER = 1024
TM_ATTN = 2048
TM_FFN = 1024
SBD_SLOTS = 16
SUB = 256
OUT_PIECE = 256
KV_BATCH = 2
VMEM_LIMIT = 56 * 1024 * 1024

F32 = jnp.float32
BF16 = jnp.bfloat16
NT_DIMS = (((1,), (1,)), ((), ()))


def _rms(x, g):
    ms = jnp.mean(x * x, axis=-1, keepdims=True)
    return x * lax.rsqrt(ms + EPS) * g


def _gelu_tanh(x):
    c = np.sqrt(2.0 / np.pi)
    hx = 0.5 * x
    return hx + hx * jnp.tanh(x * (np.float32(c) + np.float32(c * 0.044715) * (x * x)))


def _silu(x):
    return x / (1.0 + jnp.exp2(x * np.float32(-np.log2(np.e))))


def _dot(a, b):
    return jnp.dot(a, b, preferred_element_type=F32)


def _dot_nt(a, b):
    return lax.dot_general(a, b, NT_DIMS, preferred_element_type=F32)


def _mixer_kernel(x_ref, gpre_ref, win_ref, wtail_ref, wgt_ref, bg_ref, mm_ref, onorm_ref,
                  lng_ref, lnb_ref, ws_ref, bsb_ref, wout_ref, gpost_ref, o_ref, s_ref, sbd_ref):
    n_sub = TM_MIXER // SUB
    n_blocks = SUB // LANES

    @pl.when(pl.program_id(1) == 0)
    def _():
        s_ref[...] = jnp.zeros_like(s_ref)
        sbd_ref[...] = jnp.zeros_like(sbd_ref)

    state = [s_ref[GLA_DK * hh:GLA_DK * (hh + 1), :] for hh in range(GLA_HEADS)]
    ctx = [dict() for _ in range(n_sub)]
    row_chunk = lax.broadcasted_iota(jnp.int32, (SG_BLOCK, SG_BLOCK), 0) // CHUNK
    col_chunk = lax.broadcasted_iota(jnp.int32, (SG_BLOCK, SG_BLOCK), 1) // CHUNK
    causal = col_chunk <= row_chunk

    def pre_norm(k):
        c = ctx[k]
        c["xn"] = _rms(x_ref[0, SUB * k:SUB * (k + 1), :], gpre_ref[...]).astype(BF16)

    def proj_qk(k):
        c = ctx[k]
        h = _dot(c["xn"], win_ref[:, OFF_Q:OFF_V])
        c["q"] = (h[:, :GLA_KW] * np.float32(GLA_DK ** -0.5)).astype(BF16)
        c["k_t"] = h[:, GLA_KW:].T

    def proj_v(k):
        c = ctx[k]
        c["v"] = _dot(c["xn"], win_ref[:, OFF_V:OFF_G]).astype(BF16)

    def proj_sv(k):
        c = ctx[k]
        sv = _gelu_tanh(_dot(c["xn"], wtail_ref[:, T_SV:D_TAIL]))
        mu = jnp.mean(sv, axis=-1, keepdims=True)
        svc = sv - mu
        var = jnp.mean(svc * svc, axis=-1, keepdims=True)
        c["svn"] = (svc * lax.rsqrt(var + EPS) * lng_ref[...] + lnb_ref[...]).astype(BF16)

    def proj_g(k):
        c = ctx[k]
        c["gate"] = _silu(_dot(c["xn"], win_ref[:, OFF_G:OFF_A]))

    def proj_au(k):
        c = ctx[k]
        h = _dot(c["xn"], wtail_ref[:, T_A:T_SV])
        c["a_lr"] = h[:, T_A:T_U].astype(BF16)
        c["u"] = _gelu_tanh(h[:, T_U:T_SV])

    def gate_logits(k):
        c = ctx[k]
        z_t = _dot_nt(wgt_ref[...], c["a_lr"]) + jnp.concatenate([bg_ref[...]] * n_blocks, axis=1)
        la_t = (jnp.minimum(z_t, 0.0) - jnp.log(1.0 + jnp.exp(-jnp.abs(z_t)))) * np.float32(1.0 / GATE_TAU)
        c["la_hi"] = la_t.astype(BF16)
        c["la_lo"] = (la_t - c["la_hi"].astype(F32)).astype(BF16)

    def decays(k):
        c = ctx[k]
        c["ex"], c["kd"] = [], []
        for j in range(n_blocks):
            cl = slice(LANES * j, LANES * (j + 1))
            ex = jnp.exp(_dot(jnp.concatenate([c["la_hi"][:, cl], c["la_lo"][:, cl]], axis=1), mm_ref[...]))
            c["ex"].append(ex)
            c["kd"].append((c["k_t"][:, cl] * ex[:, :LANES]).astype(BF16))

    def increments(k):
        c = ctx[k]
        c["incs"] = []
        zeros = jnp.zeros((CHUNK, GLA_DV), BF16)
        for j in range(n_blocks):
            incs = []
            for hh in range(GLA_HEADS):
                rs = slice(GLA_DK * hh, GLA_DK * (hh + 1))
                cs = slice(GLA_DV * hh, GLA_DV * (hh + 1))
                v0 = c["v"][LANES * j:LANES * j + CHUNK, cs]
                v1 = c["v"][LANES * j + CHUNK:LANES * (j + 1), cs]
                v_bd = jnp.concatenate([jnp.concatenate([v0, zeros], axis=1),
                                        jnp.concatenate([zeros, v1], axis=1)], axis=0)
                incs.append(_dot(c["kd"][j][rs], v_bd))
            c["incs"].append(incs)

    def spatial_mix(k):
        c = ctx[k]
        z_cols = []
        for gi in range(SG_GROUPS):
            cs = slice(SG_GROUP_CH * gi, SG_GROUP_CH * (gi + 1))
            w_g = jnp.where(causal, ws_ref[gi], jnp.zeros((), BF16))
            rhs = jnp.concatenate([c["svn"][SG_BLOCK * j:SG_BLOCK * (j + 1), cs] for j in range(n_blocks)], axis=1)
            z = _dot(w_g, rhs)
            z_cols.append([z[:, SG_GROUP_CH * j:SG_GROUP_CH * (j + 1)] + bsb_ref[:, cs] for j in range(n_blocks)])
        z_full = jnp.concatenate(
            [jnp.concatenate([z_cols[gi][j] for gi in range(SG_GROUPS)], axis=1) for j in range(n_blocks)], axis=0)
        c["sg_out"] = (c["u"] * z_full).astype(BF16)

    def recurrence_readout(k):
        c = ctx[k]
        o_chunks = []
        for j in range(n_blocks):
            for c2 in range(2):
                ci = (k * (SUB // CHUNK) + 2 * j + c2) % SBD_SLOTS
                tot = c["ex"][j][:, LANES * (1 + c2):LANES * (2 + c2)]
                for hh in range(GLA_HEADS):
                    rs = slice(GLA_DK * hh, GLA_DK * (hh + 1))
                    cs = slice(GLA_DV * hh, GLA_DV * (hh + 1))
                    state[hh] = tot[rs] * state[hh] + c["incs"][j][hh][:, GLA_DV * c2:GLA_DV * (c2 + 1)]
                    sbd_ref[ci, rs, cs] = state[hh].astype(BF16)
                rows = slice(LANES * j + CHUNK * c2, LANES * j + CHUNK * (c2 + 1))
                o_chunks.append(_dot(c["q"][rows], sbd_ref[ci]))
        c["o"] = jnp.concatenate(o_chunks, axis=0)

    def mix_input(k):
        c = ctx[k]
        heads = []
        for hh in range(GLA_HEADS):
            oh = c["o"][:, GLA_DV * hh:GLA_DV * (hh + 1)]
            heads.append(oh * lax.rsqrt(jnp.mean(oh * oh, axis=-1, keepdims=True) + EPS))
        gla_out = (jnp.concatenate(heads, axis=-1) * onorm_ref[...] * c["gate"]).astype(BF16)
        c["lhs"] = jnp.concatenate([gla_out, c["sg_out"]], axis=1)
        c["mix"] = []

    def out_mm(k, pieces):
        c = ctx[k]
        for p in pieces:
            c["mix"].append(_dot(c["lhs"], wout_ref[:, OUT_PIECE * p:OUT_PIECE * (p + 1)]))

    def epilogue(k):
        c = ctx[k]
        mix = jnp.concatenate(c["mix"], axis=1)
        rows = slice(SUB * k, SUB * (k + 1))
        o_ref[0, rows, :] = x_ref[0, rows, :] + _rms(mix, gpost_ref[...])
        c.clear()

    pre_norm(0)
    proj_qk(0)
    proj_v(0)
    proj_sv(0)
    proj_g(0)
    proj_au(0)
    all_pieces = tuple(range(D_MODEL // OUT_PIECE))
    for k in range(n_sub):
        nxt = k + 1 < n_sub
        last = n_sub > 1 and not nxt
        gate_logits(k)
        if nxt:
            pre_norm(k + 1)
            proj_qk(k + 1)
        if last:
            out_mm(k - 1, all_pieces[:2])
        decays(k)
        if nxt:
            proj_v(k + 1)
        if last:
            out_mm(k - 1, all_pieces[2:3])
        increments(k)
        if nxt:
            proj_sv(k + 1)
        if last:
            out_mm(k - 1, all_pieces[3:])
            epilogue(k - 1)
        spatial_mix(k)
        recurrence_readout(k)
        if nxt:
            proj_g(k + 1)
            proj_au(k + 1)
        mix_input(k)
        if k + 2 == n_sub:
            continue
        out_mm(k, all_pieces)
        epilogue(k)

    for hh in range(GLA_HEADS):
        s_ref[GLA_DK * hh:GLA_DK * (hh + 1), :] = state[hh]


def _kv_kernel(mem_ref, gmem_ref, wkv_ref, k_ref, v_ref):
    nb = mem_ref.shape[0]
    memn = _rms(mem_ref[...].reshape(nb * N_MEM, D_MODEL), gmem_ref[...]).astype(BF16)
    kv = _dot(memn, wkv_ref[...])
    k_ref[...] = kv[:, :D_MODEL].astype(BF16).reshape(nb, N_MEM, D_MODEL)
    v_ref[...] = kv[:, D_MODEL:].astype(BF16).reshape(nb, N_MEM, D_MODEL)


def _attn_ffn_kernel(x_ref, gpre_ref, wq_ref, k_ref, v_ref, wo_ref, gpost_ref,
                     gfpre_ref, wgu_ref, wdown_ref, gfpost_ref, o_ref):
    n_sub = TM_FFN // SUB
    ctx = [dict() for _ in range(n_sub)]
    fctx = [dict() for _ in range(n_sub)]

    def q_proj(k):
        c = ctx[k]
        xn = _rms(x_ref[0, SUB * k:SUB * (k + 1), :], gpre_ref[...]).astype(BF16)
        c["q"] = _dot(xn, wq_ref[...]).astype(BF16)

    def scores(k):
        c = ctx[k]
        c["p"], c["denom"] = [], []
        for hh in range(X_HEADS):
            cs = slice(X_HEAD_DIM * hh, X_HEAD_DIM * (hh + 1))
            sc = _dot_nt(c["q"][:, cs], k_ref[0, :, cs])
            p = jnp.exp(sc - jnp.max(sc, axis=-1, keepdims=True))
            c["denom"].append(jnp.sum(p, axis=-1, keepdims=True))
            c["p"].append(p.astype(BF16))

    def values(k):
        c = ctx[k]
        outs = []
        for hh in range(X_HEADS):
            cs = slice(X_HEAD_DIM * hh, X_HEAD_DIM * (hh + 1))
            outs.append((_dot(c["p"][hh], v_ref[0, :, cs]) / c["denom"][hh]).astype(BF16))
        c["o"] = jnp.concatenate(outs, axis=-1)

    def out_mm(k, pieces):
        c = ctx[k]
        c.setdefault("xa", [])
        for p in pieces:
            c["xa"].append(_dot(c["o"], wo_ref[:, OUT_PIECE * p:OUT_PIECE * (p + 1)]))

    def epilogue(k):
        c = ctx[k]
        xa = jnp.concatenate(c["xa"], axis=1)
        rows = slice(SUB * k, SUB * (k + 1))
        o_ref[0, rows, :] = x_ref[0, rows, :] + _rms(xa, gpost_ref[...])
        c.clear()

    def up_proj(k):
        c = fctx[k]
        xn = _rms(o_ref[0, SUB * k:SUB * (k + 1), :], gfpre_ref[...]).astype(BF16)
        hg = _dot(xn, wgu_ref[:, :D_FF])
        hu = _dot(xn, wgu_ref[:, D_FF:])
        c["act"] = (_silu(hg) * hu).astype(BF16)

    def down_proj(k):
        c = fctx[k]
        f = _dot(c["act"], wdown_ref[...])
        rows = slice(SUB * k, SUB * (k + 1))
        o_ref[0, rows, :] = o_ref[0, rows, :] + _rms(f, gfpost_ref[...])
        c.clear()

    all_pieces = tuple(range(D_MODEL // OUT_PIECE))
    q_proj(0)
    scores(0)
    for k in range(n_sub):
        if k + 1 < n_sub:
            q_proj(k + 1)
        values(k)
        if k + 1 < n_sub:
            scores(k + 1)
        out_mm(k, all_pieces)
        epilogue(k)
        if k >= 1:
            down_proj(k - 1)
        up_proj(k)
    down_proj(n_sub - 1)


def _layer_spec(shape, layer):
    zeros = (0,) * len(shape)
    return pl.BlockSpec((None,) + tuple(shape), lambda b, t: (layer,) + zeros,
                        pipeline_mode=pl.Buffered(1))


def _const_spec(shape):
    zeros = (0,) * len(shape)
    return pl.BlockSpec(tuple(shape), lambda b, t: zeros, pipeline_mode=pl.Buffered(1))


def _tile_spec(tm):
    return pl.BlockSpec((1, tm, D_MODEL), lambda b, t: (b, t, 0))


def _params(semantics):
    return pltpu.CompilerParams(dimension_semantics=semantics, vmem_limit_bytes=VMEM_LIMIT)


def _suffix_sum_matrix():
    r = np.arange(LANES)[:, None]
    s = np.arange(LANES)[None, :]
    suffix = (r > s) & (r // CHUNK == s // CHUNK)
    tot0 = np.broadcast_to(r // CHUNK == 0, (LANES, LANES))
    tot1 = np.broadcast_to(r // CHUNK == 1, (LANES, LANES))
    m = np.concatenate([suffix, tot0, tot1], axis=1).astype(np.float32)
    return np.concatenate([m, m], axis=0)


def kernel(x, mem, norm_mix_pre, w_in, w_gate_up, b_gate, gla_onorm, sg_ln_g, sg_ln_b, w_spatial,
           b_spatial, w_out, norm_mix_post, norm_x_pre, norm_mem, w_xq, w_xkv, w_xo, norm_x_post,
           norm_ffn_pre, w_ffn_gu, w_ffn_down, norm_ffn_post):
    batch, seq, _ = x.shape
    depth = w_in.shape[0]
    assert SUB % SG_BLOCK == 0 and SG_BLOCK == 2 * CHUNK
    for tm in (TM_MIXER, TM_ATTN, TM_FFN):
        assert seq % tm == 0 and tm % SUB == 0

    w_head = w_in[:, :, OFF_Q:OFF_A].astype(BF16)
    w_tail = jnp.concatenate(
        [w_in[:, :, OFF_A:OFF_U].astype(BF16), jnp.zeros((depth, D_MODEL, GATE_PAD - GATE_RANK), BF16),
         w_in[:, :, OFF_U:D_IN].astype(BF16)], axis=-1)
    wg_t = jnp.pad(jnp.swapaxes(w_gate_up, 1, 2),
                   ((0, 0), (0, 0), (0, GATE_PAD - GATE_RANK))).astype(BF16)
    bg = jnp.broadcast_to(b_gate[:, :, None], (depth, GLA_KW, LANES)).astype(F32)
    mm = jnp.asarray(_suffix_sum_matrix(), BF16)
    ws = w_spatial.astype(BF16)
    bsb = jnp.repeat(jnp.swapaxes(b_spatial, 1, 2), SG_GROUP_CH, axis=2).astype(F32)
    wout = w_out.astype(BF16)
    wq = (w_xq * np.float32(X_HEAD_DIM ** -0.5)).astype(BF16)
    wkv = w_xkv.astype(BF16)
    wo = w_xo.astype(BF16)
    wgu = w_ffn_gu.astype(BF16)
    wdown = w_ffn_down.astype(BF16)

    def vec(a):
        return a.reshape(depth, 1, a.shape[-1]).astype(F32)

    g_mix_pre, g_mix_post = vec(norm_mix_pre), vec(norm_mix_post)
    g_x_pre, g_x_post, g_mem = vec(norm_x_pre), vec(norm_x_post), vec(norm_mem)
    g_ffn_pre, g_ffn_post = vec(norm_ffn_pre), vec(norm_ffn_post)
    onorm, lng, lnb = vec(gla_onorm), vec(sg_ln_g), vec(sg_ln_b)

    x_shape = jax.ShapeDtypeStruct(x.shape, F32)
    kv_shape = jax.ShapeDtypeStruct((depth, batch, N_MEM, D_MODEL), BF16)

    assert batch % KV_BATCH == 0
    kv_block = pl.BlockSpec((None, KV_BATCH, N_MEM, D_MODEL), lambda l, b: (l, b, 0, 0))
    k_mem, v_mem = pl.pallas_call(
        _kv_kernel,
        grid=(depth, batch // KV_BATCH),
        in_specs=[pl.BlockSpec((KV_BATCH, N_MEM, D_MODEL), lambda l, b: (b, 0, 0)),
                  pl.BlockSpec((None, 1, D_MODEL), lambda l, b: (l, 0, 0)),
                  pl.BlockSpec((None, D_MODEL, 2 * D_MODEL), lambda l, b: (l, 0, 0))],
        out_specs=[kv_block, kv_block],
        out_shape=[kv_shape, kv_shape],
        compiler_params=_params(("arbitrary", "arbitrary")),
        name="memkv",
    )(mem, g_mem, wkv)

    for l in range(depth):
        lspec = functools.partial(_layer_spec, layer=l)
        x = pl.pallas_call(
            _mixer_kernel,
            grid=(batch, seq // TM_MIXER),
            in_specs=[_tile_spec(TM_MIXER), lspec((1, D_MODEL)), lspec((D_MODEL, OFF_A)),
                      lspec((D_MODEL, D_TAIL)),
                      lspec((GLA_KW, GATE_PAD)), lspec((GLA_KW, LANES)),
                      _const_spec((2 * LANES, 3 * LANES)), lspec((1, GLA_WIDTH)), lspec((1, SG_WIDTH)),
                      lspec((1, SG_WIDTH)), lspec((SG_GROUPS, SG_BLOCK, SG_BLOCK)),
                      lspec((SG_BLOCK, SG_WIDTH)), lspec((D_MODEL, D_MODEL)), lspec((1, D_MODEL))],
            out_specs=_tile_spec(TM_MIXER),
            out_shape=x_shape,
            scratch_shapes=[pltpu.VMEM((GLA_KW, GLA_DV), F32),
                            pltpu.VMEM((SBD_SLOTS, GLA_KW, GLA_WIDTH), BF16)],
            compiler_params=_params(("arbitrary", "arbitrary")),
            name=f"mixer_l{l}",
        )(x, g_mix_pre, w_head, w_tail, wg_t, bg, mm, onorm, lng, lnb, ws, bsb, wout, g_mix_post)

        x = pl.pallas_call(
            _attn_ffn_kernel,
            grid=(batch, seq // TM_FFN),
            in_specs=[_tile_spec(TM_FFN), lspec((1, D_MODEL)), lspec((D_MODEL, D_MODEL)),
                      pl.BlockSpec((None, 1, N_MEM, D_MODEL), lambda b, t: (l, b, 0, 0)),
                      pl.BlockSpec((None, 1, N_MEM, D_MODEL), lambda b, t: (l, b, 0, 0)),
                      lspec((D_MODEL, D_MODEL)), lspec((1, D_MODEL)), lspec((1, D_MODEL)),
                      lspec((D_MODEL, 2 * D_FF)), lspec((D_FF, D_MODEL)), lspec((1, D_MODEL))],
            out_specs=_tile_spec(TM_FFN),
            out_shape=x_shape,
            compiler_params=_params(("arbitrary", "arbitrary")),
            name=f"xattn_ffn_l{l}",
        )(x, g_x_pre, wq, k_mem, v_mem, wo, g_x_post, g_ffn_pre, wgu, wdown, g_ffn_post)
    return x
```
